```python
import jax, jax.numpy as jnp
from jax import lax
import numpy as np

D_MODEL = 2048
BATCH = 2
SEQ = 8192
DEPTH = 1

HEAD_DIM = 128
N_KV_HEADS = 4
DILATED_PATTERNS = ((128, 1), (512, 4), (2048, 16))
N_PATTERNS = len(DILATED_PATTERNS)
N_Q_HEADS = N_KV_HEADS * N_PATTERNS
Q_WIDTH = N_Q_HEADS * HEAD_DIM
KV_WIDTH = N_KV_HEADS * HEAD_DIM
ATTN_OUT_WIDTH = N_KV_HEADS * HEAD_DIM
ROT_DIMS = HEAD_DIM // 4
ROPE_THETA = 500000.0
LRU_WIDTH = D_MODEL - ATTN_OUT_WIDTH
LRU_BLOCK_WIDTH = 128
LRU_BLOCKS = LRU_WIDTH // LRU_BLOCK_WIDTH
CONV_WIDTH = 4
LRU_C = 8.0
IN_PROJ_WIDTH = Q_WIDTH + 2 * KV_WIDTH + 2 * LRU_WIDTH
D_FF = 256 * ((8 * D_MODEL // 3 + 255) // 256)
D_PLE = 256
LN_EPS = 1e-5
DEEPNORM_ALPHA = (2.0 * DEPTH) ** 0.25
DEEPNORM_BETA = (8.0 * DEPTH) ** -0.25

kernel_name = 'hybrid_rglru_dilated_attn_macaron_deepnorm'


def layer_norm(x, g, b):
    xf = x.astype(jnp.float32)
    mu = xf.mean(-1, keepdims=True)
    var = jnp.square(xf - mu).mean(-1, keepdims=True)
    y = (xf - mu) * lax.rsqrt(var + LN_EPS)
    return (y * g.astype(jnp.float32) + b.astype(jnp.float32)).astype(x.dtype)


def swiglu(x, w_gate, w_up, w_down):
    return (jax.nn.silu(x @ w_gate) * (x @ w_up)) @ w_down


def partial_rotary(t, positions):
    half = ROT_DIMS // 2
    inv_freq = jnp.power(jnp.float32(ROPE_THETA), -jnp.arange(half, dtype=jnp.float32) * (2.0 / ROT_DIMS))
    ang = positions.astype(jnp.float32)[:, None, :, None] * inv_freq
    cos = jnp.cos(ang).astype(t.dtype)
    sin = jnp.sin(ang).astype(t.dtype)
    t1 = t[..., :half]
    t2 = t[..., half:ROT_DIMS]
    return jnp.concatenate([t1 * cos - t2 * sin, t2 * cos + t1 * sin, t[..., ROT_DIMS:]], axis=-1)


def dilated_window_attention(q, k, v, window, dilation):
    b, h, s, dh = q.shape
    span = window // dilation
    blk = span
    sub_len = s // dilation
    pad = (-sub_len) % blk
    nb = (sub_len + pad) // blk

    def to_blocks(t):
        t = t.reshape(b, h, sub_len, dilation, dh).swapaxes(2, 3)
        t = jnp.pad(t, ((0, 0), (0, 0), (0, 0), (0, pad), (0, 0)))
        return t.reshape(b, h, dilation, nb, blk, dh)

    def with_prev(t):
        prev = jnp.pad(t, ((0, 0), (0, 0), (0, 0), (1, 0), (0, 0), (0, 0)))[:, :, :, :-1]
        return jnp.concatenate([prev, t], axis=4)

    qb = to_blocks(q)
    kw = with_prev(to_blocks(k))
    vw = with_prev(to_blocks(v))
    scores = jnp.einsum('bhrnqe,bhrnke->bhrnqk', qb, kw,
                        preferred_element_type=jnp.float32) * (dh ** -0.5)
    qi = jnp.arange(blk)[:, None]
    ki = jnp.arange(2 * blk)[None, :]
    dist = qi + blk - ki
    band = (dist >= 0) & (dist <= span)
    not_first = (jnp.arange(nb) > 0)[:, None, None]
    mask = band[None] & (not_first | (ki >= blk)[None])
    scores = jnp.where(mask, scores, -jnp.inf)
    m = scores.max(-1, keepdims=True)
    e = jnp.exp(scores - m)
    den = e.sum(-1, keepdims=True)
    out = jnp.einsum('bhrnqk,bhrnke->bhrnqe', e, vw.astype(jnp.float32)) / den
    lse = (m + jnp.log(den))[..., 0]
    out = out.reshape(b, h, dilation, nb * blk, dh)[:, :, :, :sub_len].swapaxes(2, 3).reshape(b, h, s, dh)
    lse = lse.reshape(b, h, dilation, nb * blk)[..., :sub_len].swapaxes(2, 3).reshape(b, h, s)
    return out, lse


def rg_lru_branch(xb, yb, conv_w, conv_b, w_rgate, b_rgate, w_igate, b_igate, lam):
    b, s, c = xb.shape
    xc = lax.conv_general_dilated(xb, conv_w[:, None, :], window_strides=(1,),
                                  padding=((CONV_WIDTH - 1, 0),),
                                  dimension_numbers=('NWC', 'WIO', 'NWC'),
                                  feature_group_count=c) + conv_b
    xh = xc.reshape(b, s, LRU_BLOCKS, LRU_BLOCK_WIDTH)
    r = jax.nn.sigmoid(jnp.einsum('bsgi,gij->bsgj', xh, w_rgate).reshape(b, s, c) + b_rgate)
    i = jax.nn.sigmoid(jnp.einsum('bsgi,gij->bsgj', xh, w_igate).reshape(b, s, c) + b_igate)
    log_a = -LRU_C * jax.nn.softplus(-lam.astype(jnp.float32)) * r.astype(jnp.float32)
    a = jnp.exp(log_a)
    u = jnp.sqrt(-jnp.expm1(2.0 * log_a)) * (i * xc).astype(jnp.float32)

    def combine(left, right):
        a1, b1 = left
        a2, b2 = right
        return a1 * a2, a2 * b1 + b2

    _, hseq = lax.associative_scan(combine, (a, u), axis=1)
    return hseq.astype(xb.dtype) * jax.nn.gelu(yb)


def hybrid_mixer(h, positions, w_in, conv_w, conv_b, w_rgate, b_rgate, w_igate, b_igate, lam, w_out):
    b, s, _ = h.shape
    proj = h @ w_in
    q, k, v, xb, yb = jnp.split(proj, [Q_WIDTH, Q_WIDTH + KV_WIDTH, Q_WIDTH + 2 * KV_WIDTH,
                                       Q_WIDTH + 2 * KV_WIDTH + LRU_WIDTH], axis=-1)
    q = q.reshape(b, s, N_PATTERNS, N_KV_HEADS, HEAD_DIM).transpose(2, 0, 3, 1, 4)
    k = k.reshape(b, s, N_KV_HEADS, HEAD_DIM).transpose(0, 2, 1, 3)
    v = v.reshape(b, s, N_KV_HEADS, HEAD_DIM).transpose(0, 2, 1, 3)
    q = partial_rotary(q, positions)
    k = partial_rotary(k, positions)
    outs = []
    lses = []
    for g, (window, dilation) in enumerate(DILATED_PATTERNS):
        o, l = dilated_window_attention(q[g], k, v, window, dilation)
        outs.append(o)
        lses.append(l)
    weights = jax.nn.softmax(jnp.stack(lses), axis=0)
    attn = jnp.einsum('gbhs,gbhse->bshe', weights, jnp.stack(outs))
    attn = attn.reshape(b, s, ATTN_OUT_WIDTH).astype(h.dtype)
    rec = rg_lru_branch(xb, yb, conv_w, conv_b, w_rgate, b_rgate, w_igate, b_igate, lam)
    return jnp.concatenate([attn, rec], axis=-1) @ w_out


def setup_inputs(seed: int = 0) -> dict:
    key = jax.random.key(seed)
    ks = jax.random.split(key, 32)
    f32 = jnp.float32

    def nrm(k, shape, scale):
        return jax.random.normal(k, shape, f32) * scale

    x = jax.random.normal(ks[0], (BATCH, SEQ, D_MODEL), f32)
    p = jax.random.normal(ks[1], (DEPTH, BATCH, SEQ, D_PLE), f32)
    offset = jax.random.randint(ks[2], (BATCH, 1), 0, 1024, dtype=jnp.int32)
    positions = (jnp.arange(SEQ, dtype=jnp.int32)[None, :] + offset).astype(jnp.int32)
    a_pow = jax.random.uniform(ks[3], (DEPTH, LRU_WIDTH), f32, minval=0.9, maxval=0.999)
    a_base = a_pow ** (1.0 / LRU_C)
    lru_lambda = jnp.log(a_base) - jnp.log1p(-a_base)
    return {
        'x': x,
        'p': p,
        'positions': positions,
        'ffn1_w_gate': nrm(ks[4], (DEPTH, D_MODEL, D_FF), D_MODEL ** -0.5),
        'ffn1_w_up': nrm(ks[5], (DEPTH, D_MODEL, D_FF), D_MODEL ** -0.5),
        'ffn1_w_down': nrm(ks[6], (DEPTH, D_FF, D_MODEL), D_FF ** -0.5 * DEEPNORM_BETA),
        'ln1_g': 1.0 + nrm(ks[7], (DEPTH, D_MODEL), 0.02),
        'ln1_b': nrm(ks[8], (DEPTH, D_MODEL), 0.02),
        'w_in': nrm(ks[9], (DEPTH, D_MODEL, IN_PROJ_WIDTH), D_MODEL ** -0.5),
        'conv_w': nrm(ks[10], (DEPTH, CONV_WIDTH, LRU_WIDTH), CONV_WIDTH ** -0.5),
        'conv_b': nrm(ks[11], (DEPTH, LRU_WIDTH), 0.02),
        'w_rgate': nrm(ks[12], (DEPTH, LRU_BLOCKS, LRU_BLOCK_WIDTH, LRU_BLOCK_WIDTH), LRU_BLOCK_WIDTH ** -0.5),
        'b_rgate': nrm(ks[13], (DEPTH, LRU_WIDTH), 0.02),
        'w_igate': nrm(ks[14], (DEPTH, LRU_BLOCKS, LRU_BLOCK_WIDTH, LRU_BLOCK_WIDTH), LRU_BLOCK_WIDTH ** -0.5),
        'b_igate': nrm(ks[15], (DEPTH, LRU_WIDTH), 0.02),
        'lru_lambda': lru_lambda,
        'w_out': nrm(ks[16], (DEPTH, D_MODEL, D_MODEL), D_MODEL ** -0.5 * DEEPNORM_BETA),
        'ln2_g': 1.0 + nrm(ks[17], (DEPTH, D_MODEL), 0.02),
        'ln2_b': nrm(ks[18], (DEPTH, D_MODEL), 0.02),
        'ffn2_w_gate': nrm(ks[19], (DEPTH, D_MODEL, D_FF), D_MODEL ** -0.5),
        'ffn2_w_up': nrm(ks[20], (DEPTH, D_MODEL, D_FF), D_MODEL ** -0.5),
        'ffn2_w_down': nrm(ks[21], (DEPTH, D_FF, D_MODEL), D_FF ** -0.5 * DEEPNORM_BETA),
        'ln3_g': 1.0 + nrm(ks[22], (DEPTH, D_MODEL), 0.02),
        'ln3_b': nrm(ks[23], (DEPTH, D_MODEL), 0.02),
        'w_ple_proj': nrm(ks[24], (DEPTH, D_PLE, D_MODEL), D_PLE ** -0.5),
        'w_ple_gate': nrm(ks[25], (DEPTH, D_MODEL, D_MODEL), D_MODEL ** -0.5),
    }


def reference(x, p, positions, ffn1_w_gate, ffn1_w_up, ffn1_w_down, ln1_g, ln1_b,
              w_in, conv_w, conv_b, w_rgate, b_rgate, w_igate, b_igate, lru_lambda, w_out,
              ln2_g, ln2_b, ffn2_w_gate, ffn2_w_up, ffn2_w_down, ln3_g, ln3_b,
              w_ple_proj, w_ple_gate):
    for i in range(DEPTH):
        x = layer_norm(DEEPNORM_ALPHA * x + 0.5 * swiglu(x, ffn1_w_gate[i], ffn1_w_up[i], ffn1_w_down[i]),
                       ln1_g[i], ln1_b[i])
        mix = hybrid_mixer(x, positions, w_in[i], conv_w[i], conv_b[i], w_rgate[i], b_rgate[i],
                           w_igate[i], b_igate[i], lru_lambda[i], w_out[i])
        x = layer_norm(DEEPNORM_ALPHA * x + mix, ln2_g[i], ln2_b[i])
        x = layer_norm(DEEPNORM_ALPHA * x + 0.5 * swiglu(x, ffn2_w_gate[i], ffn2_w_up[i], ffn2_w_down[i]),
                       ln3_g[i], ln3_b[i])
        x = x + jax.nn.sigmoid(x @ w_ple_gate[i]) * (p[i] @ w_ple_proj[i])
    return x
```

```python
import functools

import jax
import jax.numpy as jnp
from jax import lax
from jax.experimental import pallas as pl
from jax.experimental.pallas import tpu as pltpu

F32 = jnp.float32
BF16 = jnp.bfloat16

HEAD_DIM = 128
N_KV_HEADS = 4
DILATIONS = (1, 4, 16)
ATTN_SPAN = 128
ROT_DIMS = HEAD_DIM // 4
ROPE_THETA = 500000.0
LRU_BLOCK_WIDTH = 128
CONV_WIDTH = 4
LRU_C = 8.0
LN_EPS = 1e-5

V7X_LANES = 128
V7X_SUBLANES = 8
V7X_VMEM_SCOPED_BYTES = 60000 * 1024


def _compiler_params(semantics, vmem_estimate_bytes):
    limit = min(int(vmem_estimate_bytes), V7X_VMEM_SCOPED_BYTES)
    return pltpu.CompilerParams(dimension_semantics=semantics, vmem_limit_bytes=limit)


def _layer_norm_rows(y, g, b):
    mu = jnp.mean(y, axis=-1, keepdims=True)
    yc = y - mu
    var = jnp.mean(yc * yc, axis=-1, keepdims=True)
    return yc * lax.rsqrt(var + LN_EPS) * g + b


def _ffn_ln_kernel(x_ref, wg_ref, wu_ref, wd_ref, g_ref, b_ref, o_ref, xb_ref, *, alpha):
    f = pl.program_id(1)

    @pl.when(f == 0)
    def _():
        xb_ref[...] = x_ref[...].astype(BF16)
        o_ref[...] = jnp.zeros_like(o_ref)

    xb = xb_ref[...]
    gate = jnp.dot(xb, wg_ref[...], preferred_element_type=F32)
    up = jnp.dot(xb, wu_ref[...], preferred_element_type=F32)
    act = (jax.nn.silu(gate) * up).astype(BF16)
    o_ref[...] += jnp.dot(act, wd_ref[...], preferred_element_type=F32)

    @pl.when(f == pl.num_programs(1) - 1)
    def _():
        y = alpha * x_ref[...] + 0.5 * o_ref[...]
        o_ref[...] = _layer_norm_rows(y, g_ref[...], b_ref[...])


def _ffn_ln(x, wg, wu, wd, g, b, *, alpha, tm=512, tf=512):
    m, d = x.shape
    dff = wg.shape[1]
    assert m % tm == 0 and dff % tf == 0
    vmem = (2 * 2 * tm * d * 4 + tm * d * 2 + 2 * 3 * d * tf * 2
            + 3 * tm * tf * 4 + tm * d * 4 + (4 << 20))
    return pl.pallas_call(
        functools.partial(_ffn_ln_kernel, alpha=alpha),
        grid=(m // tm, dff // tf),
        in_specs=[
            pl.BlockSpec((tm, d), lambda i, f: (i, 0)),
            pl.BlockSpec((d, tf), lambda i, f: (0, f)),
            pl.BlockSpec((d, tf), lambda i, f: (0, f)),
            pl.BlockSpec((tf, d), lambda i, f: (f, 0)),
            pl.BlockSpec((1, d), lambda i, f: (0, 0)),
            pl.BlockSpec((1, d), lambda i, f: (0, 0)),
        ],
        out_specs=pl.BlockSpec((tm, d), lambda i, f: (i, 0)),
        out_shape=jax.ShapeDtypeStruct((m, d), F32),
        scratch_shapes=[pltpu.VMEM((tm, d), BF16)],
        compiler_params=_compiler_params(("parallel", "arbitrary"), vmem),
        name="ffn_ln",
    )(x, wg, wu, wd, g, b)


def _in_proj_kernel(pos_ref, invf_ref, x_ref, w_ref, o_ref, xb_ref, rot_ref, *, n_rot_chunks):
    j = pl.program_id(1)

    @pl.when(j == 0)
    def _():
        xb_ref[...] = x_ref[...].astype(BF16)
        ang = pos_ref[...].astype(F32) * invf_ref[...]
        lane = lax.broadcasted_iota(jnp.int32, ang.shape, 1)
        half = ROT_DIMS // 2
        sin = jnp.sin(ang)
        rot_ref[0] = jnp.cos(ang)
        rot_ref[1] = jnp.where(lane < half, -sin, 0.0)
        rot_ref[2] = jnp.where(lane >= half, sin, 0.0)

    acc = jnp.dot(xb_ref[...], w_ref[...], preferred_element_type=F32)

    @pl.when(j < n_rot_chunks)
    def _():
        half = ROT_DIMS // 2
        for h in range(acc.shape[1] // HEAD_DIM):
            t = acc[:, h * HEAD_DIM:(h + 1) * HEAD_DIM]
            o_ref[:, h * HEAD_DIM:(h + 1) * HEAD_DIM] = (
                t * rot_ref[0]
                + pltpu.roll(t, HEAD_DIM - half, 1) * rot_ref[1]
                + pltpu.roll(t, half, 1) * rot_ref[2])

    @pl.when(j >= n_rot_chunks)
    def _():
        o_ref[...] = acc


def _in_proj(x, pos, invf, w, *, rot_width, tm=512, tn=512):
    m, d = x.shape
    n = w.shape[1]
    assert m % tm == 0 and n % tn == 0 and rot_width % tn == 0 and tn % HEAD_DIM == 0
    vmem = (2 * tm * d * 4 + tm * d * 2 + 2 * d * tn * 2 + 2 * tm * tn * 4
            + 3 * tm * V7X_LANES * 4 + 2 * tm * V7X_LANES * 4 + 2 * tm * tn * 4 + (4 << 20))
    return pl.pallas_call(
        functools.partial(_in_proj_kernel, n_rot_chunks=rot_width // tn),
        grid=(m // tm, n // tn),
        in_specs=[
            pl.BlockSpec((tm, 1), lambda i, j: (i, 0)),
            pl.BlockSpec((1, V7X_LANES), lambda i, j: (0, 0)),
            pl.BlockSpec((tm, d), lambda i, j: (i, 0)),
            pl.BlockSpec((d, tn), lambda i, j: (0, j)),
        ],
        out_specs=pl.BlockSpec((tm, tn), lambda i, j: (i, j)),
        out_shape=jax.ShapeDtypeStruct((m, n), F32),
        scratch_shapes=[pltpu.VMEM((tm, d), BF16), pltpu.VMEM((3, tm, V7X_LANES), F32)],
        compiler_params=_compiler_params(("parallel", "arbitrary"), vmem),
        name="in_proj",
    )(pos, invf, x, w)


ATTN_TILE = ATTN_SPAN * DILATIONS[-1]
ATTN_UNITS = ATTN_TILE // ATTN_SPAN


def _attn_kernel(q0_ref, q1_ref, q2_ref, kc_ref, kp_ref, vc_ref, vp_ref, o_ref,
                 qs_ref, kps_ref, kcs_ref, vps_ref, vcs_ref, acc_ref, lse_ref):
    first_tile = pl.program_id(2) == 0
    blk = ATTN_SPAN
    scale = HEAD_DIM ** -0.5
    qi = lax.broadcasted_iota(jnp.int32, (1, blk, blk), 1)
    ki = lax.broadcasted_iota(jnp.int32, (1, blk, blk), 2)
    unit = lax.broadcasted_iota(jnp.int32, (ATTN_UNITS, 1, 1), 0)
    neg_inf = -jnp.inf

    def load_rows(ref, start, size, stride):
        if stride == 1:
            return ref[pl.ds(start, size), :]
        return ref[pl.ds(start, size, stride=stride), :]

    for p, (q_ref, d) in enumerate(zip((q0_ref, q1_ref, q2_ref), DILATIONS)):
        nb = ATTN_TILE // (d * blk)
        sub = ATTN_TILE // d
        for r in range(d):
            u0 = r * nb
            qs_ref[u0:u0 + nb] = load_rows(q_ref, r, sub, d).astype(BF16).reshape(nb, blk, HEAD_DIM)
            for cur_ref, prev_ref, cs_ref, ps_ref in ((kc_ref, kp_ref, kcs_ref, kps_ref),
                                                      (vc_ref, vp_ref, vcs_ref, vps_ref)):
                cur = load_rows(cur_ref, r, sub, d).astype(BF16).reshape(nb, blk, HEAD_DIM)
                cs_ref[u0:u0 + nb] = cur
                ps_ref[u0] = load_rows(prev_ref, ATTN_TILE - blk * d + r, blk, d).astype(BF16)
                if nb > 1:
                    ps_ref[u0 + 1:u0 + nb] = cur[:nb - 1]

        q = qs_ref[...]
        sp = jnp.einsum("uqd,ukd->uqk", q, kps_ref[...], preferred_element_type=F32) * scale
        sc = jnp.einsum("uqd,ukd->uqk", q, kcs_ref[...], preferred_element_type=F32) * scale
        no_prev = jnp.logical_and(first_tile, unit % nb == 0)
        sp = jnp.where(ki >= qi, sp, neg_inf)
        sp = jnp.where(no_prev, neg_inf, sp)
        sc = jnp.where(ki <= qi, sc, neg_inf)
        mx = jnp.maximum(jnp.max(sp, axis=-1, keepdims=True), jnp.max(sc, axis=-1, keepdims=True))
        ep = jnp.exp(sp - mx)
        ec = jnp.exp(sc - mx)
        den = jnp.sum(ep, axis=-1, keepdims=True) + jnp.sum(ec, axis=-1, keepdims=True)
        out = (jnp.einsum("uqk,ukd->uqd", ep.astype(BF16), vps_ref[...], preferred_element_type=F32)
               + jnp.einsum("uqk,ukd->uqd", ec.astype(BF16), vcs_ref[...], preferred_element_type=F32)) / den
        lse = jnp.broadcast_to(mx + jnp.log(den), out.shape)
        if d == 1:
            acc_ref[p] = out.reshape(ATTN_TILE, HEAD_DIM)
            lse_ref[p] = lse.reshape(ATTN_TILE, HEAD_DIM)
        else:
            for u in range(ATTN_UNITS):
                r, jb = divmod(u, nb)
                rows = pl.ds(d * blk * jb + r, blk, stride=d)
                acc_ref[p, rows, :] = out[u]
                lse_ref[p, rows, :] = lse[u]

    lse = lse_ref[...]
    w = jnp.exp(lse - jnp.max(lse, axis=0, keepdims=True))
    merged = jnp.sum(w * acc_ref[...], axis=0) / jnp.sum(w, axis=0)
    o_ref[...] = merged.astype(o_ref.dtype)


def _attention(proj, *, batch, seq, q_col, k_col, v_col):
    m = proj.shape[0]
    nt = seq // ATTN_TILE
    assert seq % ATTN_TILE == 0 and m == batch * seq
    heads = N_KV_HEADS
    blk_spec = lambda col, prev: pl.BlockSpec(
        (ATTN_TILE, HEAD_DIM),
        (lambda b, h, n: (b * nt + jnp.maximum(n - 1, 0), col + h)) if prev
        else (lambda b, h, n: (b * nt + n, col + h)))
    tile_bytes = ATTN_TILE * HEAD_DIM * 4
    vmem = (2 * 7 * tile_bytes + 2 * tile_bytes // 2 + 5 * tile_bytes // 2 + 6 * tile_bytes
            + 10 * tile_bytes + (4 << 20))
    return pl.pallas_call(
        _attn_kernel,
        grid=(batch, heads, nt),
        in_specs=[
            blk_spec(q_col, False), blk_spec(q_col + heads, False), blk_spec(q_col + 2 * heads, False),
            blk_spec(k_col, False), blk_spec(k_col, True),
            blk_spec(v_col, False), blk_spec(v_col, True),
        ],
        out_specs=pl.BlockSpec((ATTN_TILE, HEAD_DIM), lambda b, h, n: (b * nt + n, h)),
        out_shape=jax.ShapeDtypeStruct((m, heads * HEAD_DIM), BF16),
        scratch_shapes=[pltpu.VMEM((ATTN_UNITS, ATTN_SPAN, HEAD_DIM), BF16)] * 5
        + [pltpu.VMEM((len(DILATIONS), ATTN_TILE, HEAD_DIM), F32)] * 2,
        compiler_params=_compiler_params(("parallel", "parallel", "arbitrary"), vmem),
        name="dilated_attention",
    )(proj, proj, proj, proj, proj, proj, proj)


def _linear_scan_rows(a, u):
    rows, width = a.shape
    row = lax.broadcasted_iota(jnp.int32, a.shape, 0)
    d = 1
    while d < rows:
        if d < V7X_SUBLANES:
            keep = row >= d
            a_sh = jnp.where(keep, pltpu.roll(a, d, 0), 1.0)
            u_sh = jnp.where(keep, pltpu.roll(u, d, 0), 0.0)
        else:
            a_sh = jnp.concatenate([jnp.ones((d, width), F32), a[:rows - d]], axis=0)
            u_sh = jnp.concatenate([jnp.zeros((d, width), F32), u[:rows - d]], axis=0)
        u = u + a * u_sh
        a = a * a_sh
        d *= 2
    return a, u


def _lru_kernel(x_ref, y_ref, cw_ref, cb_ref, wri_ref, br_ref, bi_ref, lam_ref, o_ref, xs_ref, h_ref):
    tl = x_ref.shape[0]
    pad = V7X_SUBLANES

    @pl.when(pl.program_id(2) == 0)
    def _():
        xs_ref[0:pad, :] = jnp.zeros((pad, xs_ref.shape[1]), F32)
        h_ref[...] = jnp.zeros_like(h_ref)

    x = x_ref[...]
    xs_ref[pad:pad + tl, :] = x
    xc = cw_ref[CONV_WIDTH - 1:CONV_WIDTH, :] * x
    for j in range(CONV_WIDTH - 1):
        shift = CONV_WIDTH - 1 - j
        xc = xc + cw_ref[j:j + 1, :] * xs_ref[pad - shift:pad - shift + tl, :]
    xc = xc + cb_ref[...]
    xs_ref[0:pad, :] = x[tl - pad:tl]

    bw = LRU_BLOCK_WIDTH
    pre = [jnp.dot(xc[:, g * bw:(g + 1) * bw].astype(BF16), wri_ref[g], preferred_element_type=F32)
           for g in range(x.shape[1] // bw)]
    r = jax.nn.sigmoid(jnp.concatenate([t[:, :bw] for t in pre], axis=1) + br_ref[...])
    i = jax.nn.sigmoid(jnp.concatenate([t[:, bw:] for t in pre], axis=1) + bi_ref[...])
    neg_lam = -lam_ref[...]
    softplus = jnp.maximum(neg_lam, 0.0) + jnp.log1p(jnp.exp(-jnp.abs(neg_lam)))
    log_a = (-LRU_C * softplus) * r
    a = jnp.exp(log_a)
    u = jnp.sqrt(1.0 - a * a) * (i * xc)
    a_cum, h = _linear_scan_rows(a, u)
    h = h + a_cum * h_ref[...]
    h_ref[...] = h[tl - 1:tl, :]
    o_ref[...] = (h * jax.nn.gelu(y_ref[...])).astype(o_ref.dtype)


def _rg_lru(proj, conv_w, conv_b, w_ri, b_r, b_i, lam, *, batch, seq, x_col, y_col, width, tl=256, tc=512):
    m = proj.shape[0]
    nt = seq // tl
    nc = width // tc
    assert seq % tl == 0 and width % tc == 0 and x_col % tc == 0 and y_col % tc == 0
    gpc = tc // LRU_BLOCK_WIDTH
    row = lambda col: pl.BlockSpec((1, tc), lambda b, c, n: (0, c))
    vmem = 2 * 2 * tl * tc * 4 + 2 * tl * tc * 2 + 2 * gpc * 128 * 256 * 2 + 16 * tl * tc * 4 + (4 << 20)
    return pl.pallas_call(
        _lru_kernel,
        grid=(batch, nc, nt),
        in_specs=[
            pl.BlockSpec((tl, tc), lambda b, c, n: (b * nt + n, x_col // tc + c)),
            pl.BlockSpec((tl, tc), lambda b, c, n: (b * nt + n, y_col // tc + c)),
            pl.BlockSpec((CONV_WIDTH, tc), lambda b, c, n: (0, c)),
            row(0),
            pl.BlockSpec((gpc, LRU_BLOCK_WIDTH, 2 * LRU_BLOCK_WIDTH), lambda b, c, n: (c, 0, 0)),
            row(0), row(0), row(0),
        ],
        out_specs=pl.BlockSpec((tl, tc), lambda b, c, n: (b * nt + n, c)),
        out_shape=jax.ShapeDtypeStruct((m, width), BF16),
        scratch_shapes=[pltpu.VMEM((tl + V7X_SUBLANES, tc), F32), pltpu.VMEM((1, tc), F32)],
        compiler_params=_compiler_params(("parallel", "parallel", "arbitrary"), vmem),
        name="rg_lru",
    )(proj, proj, conv_w, conv_b, w_ri, b_r, b_i, lam)


def _out_ln_kernel(x_ref, a_ref, r_ref, wa_ref, wr_ref, g_ref, b_ref, o_ref, *, alpha):
    mix = (jnp.dot(a_ref[...], wa_ref[...], preferred_element_type=F32)
           + jnp.dot(r_ref[...], wr_ref[...], preferred_element_type=F32))
    o_ref[...] = _layer_norm_rows(alpha * x_ref[...] + mix, g_ref[...], b_ref[...])


def _out_ln(x, attn, rec, wa, wr, g, b, *, alpha, tm=512):
    m, d = x.shape
    ka, kr = attn.shape[1], rec.shape[1]
    assert m % tm == 0
    vmem = (2 * 2 * tm * d * 4 + 2 * tm * (ka + kr) * 2 + 2 * (ka + kr) * d * 2 + 3 * tm * d * 4 + (4 << 20))
    full = lambda shape: pl.BlockSpec(shape, lambda i: (0, 0))
    return pl.pallas_call(
        functools.partial(_out_ln_kernel, alpha=alpha),
        grid=(m // tm,),
        in_specs=[
            pl.BlockSpec((tm, d), lambda i: (i, 0)),
            pl.BlockSpec((tm, ka), lambda i: (i, 0)),
            pl.BlockSpec((tm, kr), lambda i: (i, 0)),
            full((ka, d)), full((kr, d)), full((1, d)), full((1, d)),
        ],
        out_specs=pl.BlockSpec((tm, d), lambda i: (i, 0)),
        out_shape=jax.ShapeDtypeStruct((m, d), F32),
        compiler_params=_compiler_params(("parallel",), vmem),
        name="out_ln",
    )(x, attn, rec, wa, wr, g, b)


def _ple_kernel(x_ref, p_ref, wg_ref, wp_ref, o_ref):
    x = x_ref[...]
    gate = jax.nn.sigmoid(jnp.dot(x.astype(BF16), wg_ref[...], preferred_element_type=F32))
    emb = jnp.dot(p_ref[...].astype(BF16), wp_ref[...], preferred_element_type=F32)
    o_ref[...] = x + gate * emb


def _ple(x, p, wg, wp, *, tm=512):
    m, d = x.shape
    dp = p.shape[1]
    assert m % tm == 0
    vmem = 2 * 2 * tm * d * 4 + 2 * tm * dp * 4 + 2 * (d + dp) * d * 2 + 3 * tm * d * 4 + (4 << 20)
    full = lambda shape: pl.BlockSpec(shape, lambda i: (0, 0))
    return pl.pallas_call(
        _ple_kernel,
        grid=(m // tm,),
        in_specs=[
            pl.BlockSpec((tm, d), lambda i: (i, 0)),
            pl.BlockSpec((tm, dp), lambda i: (i, 0)),
            full((d, d)), full((dp, d)),
        ],
        out_specs=pl.BlockSpec((tm, d), lambda i: (i, 0)),
        out_shape=jax.ShapeDtypeStruct((m, d), F32),
        compiler_params=_compiler_params(("parallel",), vmem),
        name="ple",
    )(x, p, wg, wp)


def kernel(x, p, positions, ffn1_w_gate, ffn1_w_up, ffn1_w_down, ln1_g, ln1_b, w_in, conv_w, conv_b, w_rgate, b_rgate, w_igate, b_igate, lru_lambda, w_out, ln2_g, ln2_b, ffn2_w_gate, ffn2_w_up, ffn2_w_down, ln3_g, ln3_b, w_ple_proj, w_ple_gate):
    batch, seq, d_model = x.shape
    depth = ffn1_w_gate.shape[0]
    m = batch * seq
    kv_width = N_KV_HEADS * HEAD_DIM
    q_width = len(DILATIONS) * kv_width
    lru_width = d_model - kv_width
    alpha = (2.0 * depth) ** 0.25
    assert w_in.shape[2] == q_width + 2 * kv_width + 2 * lru_width

    half = ROT_DIMS // 2
    inv_freq = jnp.power(jnp.float32(ROPE_THETA), -jnp.arange(half, dtype=F32) * (2.0 / ROT_DIMS))
    invf = jnp.zeros((1, V7X_LANES), F32).at[0, :ROT_DIMS].set(jnp.tile(inv_freq, 2))
    pos = positions.reshape(m, 1)
    row = lambda v: v.reshape(1, -1)

    h = x.reshape(m, d_model)
    for i in range(depth):
        h = _ffn_ln(h, ffn1_w_gate[i].astype(BF16), ffn1_w_up[i].astype(BF16), ffn1_w_down[i].astype(BF16),
                    row(ln1_g[i]), row(ln1_b[i]), alpha=alpha)
        proj = _in_proj(h, pos, invf, w_in[i].astype(BF16), rot_width=q_width + kv_width)
        heads_per_col = HEAD_DIM
        attn = _attention(proj, batch=batch, seq=seq, q_col=0, k_col=q_width // heads_per_col,
                          v_col=(q_width + kv_width) // heads_per_col)
        w_ri = jnp.concatenate([w_rgate[i], w_igate[i]], axis=-1).astype(BF16)
        rec = _rg_lru(proj, conv_w[i], row(conv_b[i]), w_ri, row(b_rgate[i]), row(b_igate[i]),
                      row(lru_lambda[i]), batch=batch, seq=seq, x_col=q_width + 2 * kv_width,
                      y_col=q_width + 2 * kv_width + lru_width, width=lru_width)
        wo = w_out[i].astype(BF16)
        h = _out_ln(h, attn, rec, wo[:kv_width], wo[kv_width:], row(ln2_g[i]), row(ln2_b[i]), alpha=alpha)
        h = _ffn_ln(h, ffn2_w_gate[i].astype(BF16), ffn2_w_up[i].astype(BF16), ffn2_w_down[i].astype(BF16),
                    row(ln3_g[i]), row(ln3_b[i]), alpha=alpha)
        h = _ple(h, p[i].reshape(m, -1), w_ple_gate[i].astype(BF16), w_ple_proj[i].astype(BF16))
    return h.reshape(batch, seq, d_model)
```

```python
import functools

import jax
import jax.numpy as jnp
from jax import lax
from jax.experimental import pallas as pl
from jax.experimental.pallas import tpu as pltpu

F32 = jnp.float32
BF16 = jnp.bfloat16

HEAD_DIM = 128
N_KV_HEADS = 4
DILATIONS = (1, 4, 16)
ATTN_SPAN = 128
ROT_DIMS = HEAD_DIM // 4
ROPE_THETA = 500000.0
LRU_BLOCK_WIDTH = 128
CONV_WIDTH = 4
LRU_C = 8.0
LN_EPS = 1e-5

V7X_LANES = 128
V7X_SUBLANES = 8
V7X_VMEM_SCOPED_BYTES = 60000 * 1024

ATTN_TILE = ATTN_SPAN * DILATIONS[-1]
ATTN_UNITS = ATTN_TILE // ATTN_SPAN
QKV_ROWS = ATTN_SPAN * DILATIONS[1]


def _compiler_params(semantics, vmem_estimate_bytes):
    limit = min(int(vmem_estimate_bytes), V7X_VMEM_SCOPED_BYTES)
    return pltpu.CompilerParams(dimension_semantics=semantics, vmem_limit_bytes=limit)


def _layer_norm_rows(y, g, b):
    mu = jnp.mean(y, axis=-1, keepdims=True)
    yc = y - mu
    var = jnp.mean(yc * yc, axis=-1, keepdims=True)
    return yc * lax.rsqrt(var + LN_EPS) * g + b


def _ffn_ln_kernel(x_ref, wg_ref, wu_ref, wd_ref, g_ref, b_ref, o_ref, *, alpha):
    f = pl.program_id(1)

    @pl.when(f == 0)
    def _():
        o_ref[...] = jnp.zeros_like(o_ref)

    xb = x_ref[...].astype(BF16)
    gate = jnp.dot(xb, wg_ref[...], preferred_element_type=F32)
    up = jnp.dot(xb, wu_ref[...], preferred_element_type=F32)
    act = (jax.nn.silu(gate) * up).astype(BF16)
    o_ref[...] += jnp.dot(act, wd_ref[...], preferred_element_type=F32)

    @pl.when(f == pl.num_programs(1) - 1)
    def _():
        o_ref[...] = _layer_norm_rows(alpha * x_ref[...] + 0.5 * o_ref[...], g_ref[...], b_ref[...])


def _ffn_ln(x, wg, wu, wd, g, b, *, alpha, tm=1024, tf=512):
    m, d = x.shape
    dff = wg.shape[1]
    assert m % tm == 0 and dff % tf == 0
    vmem = (2 * tm * d * (4 + 4) + tm * d * 2 + 2 * 3 * d * tf * 2
            + 3 * tm * tf * 4 + (6 << 20))
    return pl.pallas_call(
        functools.partial(_ffn_ln_kernel, alpha=alpha),
        grid=(m // tm, dff // tf),
        in_specs=[
            pl.BlockSpec((tm, d), lambda i, f: (i, 0)),
            pl.BlockSpec((d, tf), lambda i, f: (0, f)),
            pl.BlockSpec((d, tf), lambda i, f: (0, f)),
            pl.BlockSpec((tf, d), lambda i, f: (f, 0)),
            pl.BlockSpec((1, d), lambda i, f: (0, 0)),
            pl.BlockSpec((1, d), lambda i, f: (0, 0)),
        ],
        out_specs=pl.BlockSpec((tm, d), lambda i, f: (i, 0)),
        out_shape=jax.ShapeDtypeStruct((m, d), F32),
        compiler_params=_compiler_params(("parallel", "arbitrary"), vmem),
        name="ffn_ln",
    )(x, wg, wu, wd, g, b)


def _qkv_proj_kernel(pos_ref, invf_ref, x_ref, w_ref,
                     q1_ref, q4_ref, q16_ref, k1_ref, k4_ref, k16_ref, v1_ref, v4_ref, v16_ref, slab_ref):
    tm = x_ref.shape[0]
    heads = N_KV_HEADS
    half = ROT_DIMS // 2
    kvw = heads * HEAD_DIM
    x = x_ref[...].astype(BF16)

    ang = pos_ref[...].astype(F32) * invf_ref[...]
    lane = lax.broadcasted_iota(jnp.int32, ang.shape, 1)
    sin = jnp.sin(ang)
    cos = jnp.cos(ang)
    sin_lo = jnp.where(lane < half, -sin, 0.0)
    sin_hi = jnp.where(lane >= half, sin, 0.0)

    def rotate(t):
        return t * cos + pltpu.roll(t, HEAD_DIM - half, 1) * sin_lo + pltpu.roll(t, half, 1) * sin_hi

    def write_units(t, h, slab, o1_ref, o4_ref, o16_ref):
        if o1_ref is not None:
            for jb in range(tm // ATTN_SPAN):
                o1_ref[jb, h] = t[jb * ATTN_SPAN:(jb + 1) * ATTN_SPAN].astype(BF16)
        if o4_ref is None and o16_ref is None:
            return
        slab_ref[slab] = t
        for o_ref, d in ((o4_ref, DILATIONS[1]), (o16_ref, DILATIONS[2])):
            if o_ref is not None:
                for r in range(d):
                    o_ref[r, h] = slab_ref[slab, pl.ds(r, tm // d, stride=d), :].astype(BF16)

    groups = (
        (True, (q1_ref, None, None)), (True, (None, q4_ref, None)), (True, (None, None, q16_ref)),
        (True, (k1_ref, k4_ref, k16_ref)), (False, (v1_ref, v4_ref, v16_ref)),
    )
    for j, (rot, outs) in enumerate(groups):
        acc = jnp.dot(x, w_ref[:, j * kvw:(j + 1) * kvw], preferred_element_type=F32)
        for h in range(heads):
            t = acc[:, h * HEAD_DIM:(h + 1) * HEAD_DIM]
            write_units(rotate(t) if rot else t, h, (j * heads + h) % slab_ref.shape[0], *outs)


def _qkv_proj(x, pos, invf, w):
    m, d = x.shape
    tm = QKV_ROWS
    heads = N_KV_HEADS
    n = w.shape[1]
    per_tile = ATTN_TILE // tm
    assert m % ATTN_TILE == 0 and n == 5 * heads * HEAD_DIM
    units = lambda rows: jax.ShapeDtypeStruct((rows // ATTN_SPAN, heads, ATTN_SPAN, HEAD_DIM), BF16)
    flat_spec = pl.BlockSpec((tm // ATTN_SPAN, heads, ATTN_SPAN, HEAD_DIM), lambda i: (i, 0, 0, 0))
    wide_spec = pl.BlockSpec((DILATIONS[2], heads, tm // DILATIONS[2], HEAD_DIM),
                             lambda i: (i // per_tile, 0, i % per_tile, 0))
    out_specs = [flat_spec, flat_spec, wide_spec] * 3
    n_slabs = 4
    vmem = (2 * tm * d * 4 + tm * d * 2 + 2 * d * n * 2 + 2 * 9 * tm * heads * HEAD_DIM * 2
            + n_slabs * tm * HEAD_DIM * 4 + 8 * tm * heads * HEAD_DIM * 4 + (6 << 20))
    return pl.pallas_call(
        _qkv_proj_kernel,
        grid=(m // tm,),
        in_specs=[
            pl.BlockSpec((tm, 1), lambda i: (i, 0)),
            pl.BlockSpec((1, V7X_LANES), lambda i: (0, 0)),
            pl.BlockSpec((tm, d), lambda i: (i, 0)),
            pl.BlockSpec((d, n), lambda i: (0, 0)),
        ],
        out_specs=out_specs,
        out_shape=[units(m)] * 9,
        scratch_shapes=[pltpu.VMEM((n_slabs, tm, HEAD_DIM), F32)],
        compiler_params=_compiler_params(("parallel",), vmem),
        name="qkv_proj",
    )(pos, invf, x, w)


def _matmul_kernel(x_ref, w_ref, o_ref):
    o_ref[...] = jnp.dot(x_ref[...].astype(BF16), w_ref[...], preferred_element_type=F32).astype(o_ref.dtype)


def _matmul(x, w, *, out_dtype, tm=1024, tn=1024):
    m, k = x.shape
    n = w.shape[1]
    tn = min(tn, n)
    assert m % tm == 0 and n % tn == 0
    vmem = 2 * (tm * k * 4 + k * tn * 2 + tm * tn * 4) + tm * k * 2 + tm * tn * 4 + (4 << 20)
    return pl.pallas_call(
        _matmul_kernel,
        grid=(m // tm, n // tn),
        in_specs=[pl.BlockSpec((tm, k), lambda i, j: (i, 0)), pl.BlockSpec((k, tn), lambda i, j: (0, j))],
        out_specs=pl.BlockSpec((tm, tn), lambda i, j: (i, j)),
        out_shape=jax.ShapeDtypeStruct((m, n), out_dtype),
        compiler_params=_compiler_params(("parallel", "arbitrary"), vmem),
        name="xy_proj",
    )(x, w)


def _attn_kernel(q1_ref, q4_ref, q16_ref, k1c_ref, k4c_ref, k16c_ref, k1p_ref, k4p_ref, k16p_ref,
                 v1c_ref, v4c_ref, v16c_ref, v1p_ref, v4p_ref, v16p_ref, o_ref, acc_ref, lse_ref):
    first_tile = pl.program_id(2) == 0
    blk = ATTN_SPAN
    nu = ATTN_UNITS
    scale = HEAD_DIM ** -0.5
    qi = lax.broadcasted_iota(jnp.int32, (1, blk, blk), 1)
    ki = lax.broadcasted_iota(jnp.int32, (1, blk, blk), 2)
    unit = lax.broadcasted_iota(jnp.int32, (nu, 1, 1), 0)
    neg_inf = -jnp.inf
    qk = lambda a, b: jnp.einsum("uqd,ukd->uqk", a, b, preferred_element_type=F32)
    pv = lambda a, b: jnp.einsum("uqk,ukd->uqd", a, b, preferred_element_type=F32)

    def with_prev(fn, lhs, prev_ref, cur_ref, d):
        head = fn(lhs[:d], prev_ref[...])
        if d == nu:
            return head
        return jnp.concatenate([head, fn(lhs[d:], cur_ref[0:nu - d])], axis=0)

    refs = ((q1_ref, k1c_ref, k1p_ref, v1c_ref, v1p_ref), (q4_ref, k4c_ref, k4p_ref, v4c_ref, v4p_ref),
            (q16_ref, k16c_ref, k16p_ref, v16c_ref, v16p_ref))
    for p, (d, (q_ref, kc_ref, kp_ref, vc_ref, vp_ref)) in enumerate(zip(DILATIONS, refs)):
        q = q_ref[...]
        sp = with_prev(qk, q, kp_ref, kc_ref, d) * scale
        sc = qk(q, kc_ref[...]) * scale
        sp = jnp.where(ki >= qi, sp, neg_inf)
        sp = jnp.where(jnp.logical_and(first_tile, unit < d), neg_inf, sp)
        sc = jnp.where(ki <= qi, sc, neg_inf)
        mx = jnp.max(jnp.maximum(sp, sc), axis=-1, keepdims=True)
        ep = jnp.exp(sp - mx)
        ec = jnp.exp(sc - mx)
        den = jnp.sum(ep + ec, axis=-1, keepdims=True)
        out = (with_prev(pv, ep.astype(BF16), vp_ref, vc_ref, d) + pv(ec.astype(BF16), vc_ref[...])) / den
        lse = jnp.broadcast_to(mx + jnp.log(den), out.shape)
        if d == 1:
            acc_ref[p] = out.reshape(ATTN_TILE, HEAD_DIM)
            lse_ref[p] = lse.reshape(ATTN_TILE, HEAD_DIM)
        else:
            for u in range(nu):
                jb, r = divmod(u, d) if d < nu else (0, u)
                rows = pl.ds(d * blk * jb + r, blk, stride=d)
                acc_ref[p, rows, :] = out[u]
                lse_ref[p, rows, :] = lse[u]

    lse = lse_ref[...]
    w = jnp.exp(lse - jnp.max(lse, axis=0, keepdims=True))
    merged = jnp.sum(w * acc_ref[...], axis=0) / jnp.sum(w, axis=0)
    o_ref[...] = merged.astype(o_ref.dtype)


def _attention(qkv, *, batch, seq):
    nt = seq // ATTN_TILE
    heads = N_KV_HEADS
    nu = ATTN_UNITS
    q1, q4, q16, k1, k4, k16, v1, v4, v16 = qkv

    def cur():
        return pl.BlockSpec((nu, None, ATTN_SPAN, HEAD_DIM), lambda b, h, n: (b * nt + n, h, 0, 0))

    def prev(d):
        per = nu // d
        return pl.BlockSpec((d, None, ATTN_SPAN, HEAD_DIM),
                            lambda b, h, n: (jnp.maximum((b * nt + n) * per - 1, 0), h, 0, 0))

    unit_bytes = ATTN_SPAN * HEAD_DIM * 2
    tile_f32 = ATTN_TILE * HEAD_DIM * 4
    vmem = 2 * (9 * nu + 2 * sum(DILATIONS)) * unit_bytes + 2 * tile_f32 // 2 + 6 * tile_f32 + 12 * tile_f32 + (4 << 20)
    return pl.pallas_call(
        _attn_kernel,
        grid=(batch, heads, nt),
        in_specs=[cur(), cur(), cur(), cur(), cur(), cur(), prev(1), prev(4), prev(16),
                  cur(), cur(), cur(), prev(1), prev(4), prev(16)],
        out_specs=pl.BlockSpec((ATTN_TILE, HEAD_DIM), lambda b, h, n: (b * nt + n, h)),
        out_shape=jax.ShapeDtypeStruct((batch * seq, heads * HEAD_DIM), BF16),
        scratch_shapes=[pltpu.VMEM((len(DILATIONS), ATTN_TILE, HEAD_DIM), F32)] * 2,
        compiler_params=_compiler_params(("parallel", "parallel", "arbitrary"), vmem),
        name="dilated_attention",
    )(q1, q4, q16, k1, k4, k16, k1, k4, k16, v1, v4, v16, v1, v4, v16)


def _linear_scan_rows(a, u, h0):
    rows, width = a.shape
    sub = V7X_SUBLANES
    groups = rows // sub
    a = a.reshape(groups, sub, width)
    u = u.reshape(groups, sub, width)
    row = lax.broadcasted_iota(jnp.int32, a.shape, 1)
    d = 1
    while d < sub:
        keep = row >= d
        u = u + a * jnp.where(keep, pltpu.roll(u, d, 1), 0.0)
        a = a * jnp.where(keep, pltpu.roll(a, d, 1), 1.0)
        d *= 2
    out = []
    h = h0
    for g in range(groups):
        hg = u[g] + a[g] * h
        out.append(hg)
        h = hg[sub - 1:sub, :]
    return jnp.concatenate(out, axis=0), h


def _gelu_tanh(y):
    z = 0.7978845608028654 * (y + 0.044715 * (y * y * y))
    return y * jax.nn.sigmoid(2.0 * z)


def _lru_kernel(x_ref, y_ref, cw_ref, cb_ref, wri_ref, br_ref, bi_ref, lam_ref, o_ref, xs_ref, h_ref):
    tl = x_ref.shape[0]
    pad = V7X_SUBLANES

    @pl.when(pl.program_id(2) == 0)
    def _():
        xs_ref[0:pad, :] = jnp.zeros((pad, xs_ref.shape[1]), F32)
        h_ref[...] = jnp.zeros_like(h_ref)

    x = x_ref[...]
    xs_ref[pad:pad + tl, :] = x
    xc = cw_ref[CONV_WIDTH - 1:CONV_WIDTH, :] * x
    for j in range(CONV_WIDTH - 1):
        shift = CONV_WIDTH - 1 - j
        xc = xc + cw_ref[j:j + 1, :] * xs_ref[pad - shift:pad - shift + tl, :]
    xc = xc + cb_ref[...]
    xs_ref[0:pad, :] = x[tl - pad:tl]

    bw = LRU_BLOCK_WIDTH
    pre = [jnp.dot(xc[:, g * bw:(g + 1) * bw].astype(BF16), wri_ref[g], preferred_element_type=F32)
           for g in range(x.shape[1] // bw)]
    r = jax.nn.sigmoid(jnp.concatenate([t[:, :bw] for t in pre], axis=1) + br_ref[...])
    i = jax.nn.sigmoid(jnp.concatenate([t[:, bw:] for t in pre], axis=1) + bi_ref[...])
    neg_lam = -lam_ref[...]
    softplus = jnp.maximum(neg_lam, 0.0) + jnp.log1p(jnp.exp(-jnp.abs(neg_lam)))
    log_a = (-LRU_C * softplus) * r
    a = jnp.exp(log_a)
    one_m_a2 = 1.0 - a * a
    root = jnp.where(one_m_a2 > 0.0, one_m_a2 * lax.rsqrt(one_m_a2), 0.0)
    u = root * (i * xc)
    h, h_ref[...] = _linear_scan_rows(a, u, h_ref[...])
    o_ref[...] = (h * _gelu_tanh(y_ref[...])).astype(o_ref.dtype)


def _rg_lru(xy, conv_w, conv_b, w_ri, b_r, b_i, lam, *, batch, seq, tl=256, tc=512):
    m = xy.shape[0]
    width = xy.shape[1] // 2
    nt = seq // tl
    nc = width // tc
    assert seq % tl == 0 and width % tc == 0
    gpc = tc // LRU_BLOCK_WIDTH
    row = pl.BlockSpec((1, tc), lambda b, c, n: (0, c))
    vmem = 2 * 2 * tl * tc * 4 + 2 * tl * tc * 2 + 2 * gpc * 128 * 256 * 2 + 16 * tl * tc * 4 + (4 << 20)
    return pl.pallas_call(
        _lru_kernel,
        grid=(batch, nc, nt),
        in_specs=[
            pl.BlockSpec((tl, tc), lambda b, c, n: (b * nt + n, c)),
            pl.BlockSpec((tl, tc), lambda b, c, n: (b * nt + n, nc + c)),
            pl.BlockSpec((CONV_WIDTH, tc), lambda b, c, n: (0, c)),
            row,
            pl.BlockSpec((gpc, LRU_BLOCK_WIDTH, 2 * LRU_BLOCK_WIDTH), lambda b, c, n: (c, 0, 0)),
            row, row, row,
        ],
        out_specs=pl.BlockSpec((tl, tc), lambda b, c, n: (b * nt + n, c)),
        out_shape=jax.ShapeDtypeStruct((m, width), BF16),
        scratch_shapes=[pltpu.VMEM((tl + V7X_SUBLANES, tc), F32), pltpu.VMEM((1, tc), F32)],
        compiler_params=_compiler_params(("parallel", "parallel", "arbitrary"), vmem),
        name="rg_lru",
    )(xy, xy, conv_w, conv_b, w_ri, b_r, b_i, lam)


def _out_ln_kernel(x_ref, a_ref, r_ref, wa_ref, wr_ref, g_ref, b_ref, o_ref, *, alpha):
    mix = (jnp.dot(a_ref[...], wa_ref[...], preferred_element_type=F32)
           + jnp.dot(r_ref[...], wr_ref[...], preferred_element_type=F32))
    o_ref[...] = _layer_norm_rows(alpha * x_ref[...] + mix, g_ref[...], b_ref[...])


def _out_ln(x, attn, rec, wa, wr, g, b, *, alpha, tm=512):
    m, d = x.shape
    ka, kr = attn.shape[1], rec.shape[1]
    assert m % tm == 0
    vmem = (2 * 2 * tm * d * 4 + 2 * tm * (ka + kr) * 2 + 2 * (ka + kr) * d * 2 + 3 * tm * d * 4 + (4 << 20))
    full = lambda shape: pl.BlockSpec(shape, lambda i: (0, 0))
    return pl.pallas_call(
        functools.partial(_out_ln_kernel, alpha=alpha),
        grid=(m // tm,),
        in_specs=[
            pl.BlockSpec((tm, d), lambda i: (i, 0)),
            pl.BlockSpec((tm, ka), lambda i: (i, 0)),
            pl.BlockSpec((tm, kr), lambda i: (i, 0)),
            full((ka, d)), full((kr, d)), full((1, d)), full((1, d)),
        ],
        out_specs=pl.BlockSpec((tm, d), lambda i: (i, 0)),
        out_shape=jax.ShapeDtypeStruct((m, d), F32),
        compiler_params=_compiler_params(("parallel",), vmem),
        name="out_ln",
    )(x, attn, rec, wa, wr, g, b)


def _ple_kernel(x_ref, p_ref, wg_ref, wp_ref, o_ref):
    x = x_ref[...]
    gate = jax.nn.sigmoid(jnp.dot(x.astype(BF16), wg_ref[...], preferred_element_type=F32))
    emb = jnp.dot(p_ref[...].astype(BF16), wp_ref[...], preferred_element_type=F32)
    o_ref[...] = x + gate * emb


def _ple(x, p, wg, wp, *, tm=512):
    m, d = x.shape
    dp = p.shape[1]
    assert m % tm == 0
    vmem = 2 * 2 * tm * d * 4 + 2 * tm * dp * 4 + 2 * (d + dp) * d * 2 + 3 * tm * d * 4 + (4 << 20)
    full = lambda shape: pl.BlockSpec(shape, lambda i: (0, 0))
    return pl.pallas_call(
        _ple_kernel,
        grid=(m // tm,),
        in_specs=[
            pl.BlockSpec((tm, d), lambda i: (i, 0)),
            pl.BlockSpec((tm, dp), lambda i: (i, 0)),
            full((d, d)), full((dp, d)),
        ],
        out_specs=pl.BlockSpec((tm, d), lambda i: (i, 0)),
        out_shape=jax.ShapeDtypeStruct((m, d), F32),
        compiler_params=_compiler_params(("parallel",), vmem),
        name="ple",
    )(x, p, wg, wp)


def kernel(x, p, positions, ffn1_w_gate, ffn1_w_up, ffn1_w_down, ln1_g, ln1_b, w_in, conv_w, conv_b, w_rgate, b_rgate, w_igate, b_igate, lru_lambda, w_out, ln2_g, ln2_b, ffn2_w_gate, ffn2_w_up, ffn2_w_down, ln3_g, ln3_b, w_ple_proj, w_ple_gate):
    batch, seq, d_model = x.shape
    depth = ffn1_w_gate.shape[0]
    m = batch * seq
    kv_width = N_KV_HEADS * HEAD_DIM
    qkv_width = (len(DILATIONS) + 2) * kv_width
    lru_width = d_model - kv_width
    alpha = (2.0 * depth) ** 0.25
    assert w_in.shape[2] == qkv_width + 2 * lru_width

    half = ROT_DIMS // 2
    inv_freq = jnp.power(jnp.float32(ROPE_THETA), -jnp.arange(half, dtype=F32) * (2.0 / ROT_DIMS))
    invf = jnp.zeros((1, V7X_LANES), F32).at[0, :ROT_DIMS].set(jnp.tile(inv_freq, 2))
    pos = positions.reshape(m, 1)
    row = lambda v: v.reshape(1, -1)

    h = x.reshape(m, d_model)
    for i in range(depth):
        h = _ffn_ln(h, ffn1_w_gate[i].astype(BF16), ffn1_w_up[i].astype(BF16), ffn1_w_down[i].astype(BF16),
                    row(ln1_g[i]), row(ln1_b[i]), alpha=alpha)
        qkv = _qkv_proj(h, pos, invf, w_in[i, :, :qkv_width].astype(BF16))
        xy = _matmul(h, w_in[i, :, qkv_width:].astype(BF16), out_dtype=F32)
        attn = _attention(qkv, batch=batch, seq=seq)
        w_ri = jnp.concatenate([w_rgate[i], w_igate[i]], axis=-1).astype(BF16)
        rec = _rg_lru(xy, conv_w[i], row(conv_b[i]), w_ri, row(b_rgate[i]), row(b_igate[i]),
                      row(lru_lambda[i]), batch=batch, seq=seq)
        wo = w_out[i].astype(BF16)
        h = _out_ln(h, attn, rec, wo[:kv_width], wo[kv_width:], row(ln2_g[i]), row(ln2_b[i]), alpha=alpha)
        h = _ffn_ln(h, ffn2_w_gate[i].astype(BF16), ffn2_w_up[i].astype(BF16), ffn2_w_down[i].astype(BF16),
                    row(ln3_g[i]), row(ln3_b[i]), alpha=alpha)
        h = _ple(h, p[i].reshape(m, -1), w_ple_gate[i].astype(BF16), w_ple_proj[i].astype(BF16))
    return h.reshape(batch, seq, d_model)
```

```python
import functools

import jax
import jax.numpy as jnp
from jax import lax
from jax.experimental import pallas as pl
from jax.experimental.pallas import tpu as pltpu

F32 = jnp.float32
BF16 = jnp.bfloat16

HEAD_DIM = 128
N_KV_HEADS = 4
DILATIONS = (1, 4, 16)
ATTN_SPAN = 128
ROT_DIMS = HEAD_DIM // 4
ROPE_THETA = 500000.0
LRU_BLOCK_WIDTH = 128
CONV_WIDTH = 4
LRU_C = 8.0
LN_EPS = 1e-5

V7X_LANES = 128
V7X_SUBLANES = 8
V7X_VMEM_SCOPED_BYTES = 60000 * 1024

ATTN_TILE = ATTN_SPAN * DILATIONS[-1]
ATTN_UNITS = ATTN_TILE // ATTN_SPAN
QKV_ROWS = ATTN_SPAN * DILATIONS[1]


def _compiler_params(semantics, vmem_estimate_bytes):
    limit = min(int(vmem_estimate_bytes), V7X_VMEM_SCOPED_BYTES)
    return pltpu.CompilerParams(dimension_semantics=semantics, vmem_limit_bytes=limit)


def _layer_norm_rows(y, g, b):
    mu = jnp.mean(y, axis=-1, keepdims=True)
    yc = y - mu
    var = jnp.mean(yc * yc, axis=-1, keepdims=True)
    return yc * lax.rsqrt(var + LN_EPS) * g + b


def _ffn_ln_kernel(x_ref, wg_ref, wu_ref, wd_ref, g_ref, b_ref, o_ref, *, alpha, ln_row_chunks):
    f = pl.program_id(1)
    last = pl.num_programs(1) - 1

    @pl.when(f == 0)
    def _():
        o_ref[...] = jnp.zeros_like(o_ref)

    def accumulated(rows):
        xb = x_ref[rows, :].astype(BF16)
        gate = jnp.dot(xb, wg_ref[...], preferred_element_type=F32)
        up = jnp.dot(xb, wu_ref[...], preferred_element_type=F32)
        act = (jax.nn.silu(gate) * up).astype(BF16)
        return o_ref[rows, :] + jnp.dot(act, wd_ref[...], preferred_element_type=F32)

    @pl.when(f < last)
    def _():
        o_ref[...] = accumulated(slice(None))

    @pl.when(f == last)
    def _():
        rows_per = x_ref.shape[0] // ln_row_chunks
        for c in range(ln_row_chunks):
            rows = slice(c * rows_per, (c + 1) * rows_per)
            y = alpha * x_ref[rows, :] + 0.5 * accumulated(rows)
            o_ref[rows, :] = _layer_norm_rows(y, g_ref[...], b_ref[...])


def _ffn_ln(x, wg, wu, wd, g, b, *, alpha, tm=1024, tf=512, ln_row_chunks=4):
    m, d = x.shape
    dff = wg.shape[1]
    assert m % tm == 0 and dff % tf == 0
    vmem = (2 * tm * d * (4 + 4) + tm * d * 2 + 2 * 3 * d * tf * 2
            + 3 * tm * tf * 4 + (6 << 20))
    return pl.pallas_call(
        functools.partial(_ffn_ln_kernel, alpha=alpha, ln_row_chunks=ln_row_chunks),
        grid=(m // tm, dff // tf),
        in_specs=[
            pl.BlockSpec((tm, d), lambda i, f: (i, 0)),
            pl.BlockSpec((d, tf), lambda i, f: (0, f)),
            pl.BlockSpec((d, tf), lambda i, f: (0, f)),
            pl.BlockSpec((tf, d), lambda i, f: (f, 0)),
            pl.BlockSpec((1, d), lambda i, f: (0, 0)),
            pl.BlockSpec((1, d), lambda i, f: (0, 0)),
        ],
        out_specs=pl.BlockSpec((tm, d), lambda i, f: (i, 0)),
        out_shape=jax.ShapeDtypeStruct((m, d), F32),
        compiler_params=_compiler_params(("parallel", "arbitrary"), vmem),
        name="ffn_ln",
    )(x, wg, wu, wd, g, b)


def _qkv_proj_kernel(pos_ref, invf_ref, x_ref, w_ref,
                     q1_ref, q4_ref, q16_ref, k1_ref, k4_ref, k16_ref, v1_ref, v4_ref, v16_ref, xb_ref,
                     slab_ref):
    tm = x_ref.shape[0]
    heads = N_KV_HEADS
    half = ROT_DIMS // 2
    kvw = heads * HEAD_DIM
    x = x_ref[...].astype(BF16)
    xb_ref[...] = x

    ang = pos_ref[...].astype(F32) * invf_ref[...]
    lane = lax.broadcasted_iota(jnp.int32, ang.shape, 1)
    sin = jnp.sin(ang)
    cos = jnp.cos(ang)
    sin_lo = jnp.where(lane < half, -sin, 0.0)
    sin_hi = jnp.where(lane >= half, sin, 0.0)

    def rotate(t):
        return t * cos + pltpu.roll(t, HEAD_DIM - half, 1) * sin_lo + pltpu.roll(t, half, 1) * sin_hi

    def write_units(t, h, slab, o1_ref, o4_ref, o16_ref):
        if o1_ref is not None:
            for jb in range(tm // ATTN_SPAN):
                o1_ref[jb, h] = t[jb * ATTN_SPAN:(jb + 1) * ATTN_SPAN].astype(BF16)
        if o4_ref is None and o16_ref is None:
            return
        slab_ref[slab] = t
        for o_ref, d in ((o4_ref, DILATIONS[1]), (o16_ref, DILATIONS[2])):
            if o_ref is not None:
                for r in range(d):
                    o_ref[r, h] = slab_ref[slab, pl.ds(r, tm // d, stride=d), :].astype(BF16)

    groups = (
        (0, True, (q1_ref, None, None)), (1, True, (None, q4_ref, None)), (2, True, (None, None, q16_ref)),
        (3, True, (k1_ref, k4_ref, k16_ref)), (4, False, (v1_ref, v4_ref, v16_ref)),
    )
    for step, (j, rot, outs) in enumerate(groups):
        acc = jnp.dot(x, w_ref[:, j * kvw:(j + 1) * kvw], preferred_element_type=F32)
        for h in range(heads):
            t = acc[:, h * HEAD_DIM:(h + 1) * HEAD_DIM]
            write_units(rotate(t) if rot else t, h, (step * heads + h) % slab_ref.shape[0], *outs)


def _qkv_proj(x, pos, invf, w):
    m, d = x.shape
    tm = QKV_ROWS
    heads = N_KV_HEADS
    n = w.shape[1]
    per_tile = ATTN_TILE // tm
    assert m % ATTN_TILE == 0 and n == 5 * heads * HEAD_DIM
    units = lambda rows: jax.ShapeDtypeStruct((rows // ATTN_SPAN, heads, ATTN_SPAN, HEAD_DIM), BF16)
    flat_spec = pl.BlockSpec((tm // ATTN_SPAN, heads, ATTN_SPAN, HEAD_DIM), lambda i: (i, 0, 0, 0))
    wide_spec = pl.BlockSpec((DILATIONS[2], heads, tm // DILATIONS[2], HEAD_DIM),
                             lambda i: (i // per_tile, 0, i % per_tile, 0))
    out_specs = [flat_spec, flat_spec, wide_spec] * 3 + [pl.BlockSpec((tm, d), lambda i: (i, 0))]
    n_slabs = 4
    vmem = (2 * tm * d * 4 + tm * d * 2 + 2 * d * n * 2 + 2 * 9 * tm * heads * HEAD_DIM * 2
            + n_slabs * tm * HEAD_DIM * 4 + 8 * tm * heads * HEAD_DIM * 4 + (6 << 20))
    return pl.pallas_call(
        _qkv_proj_kernel,
        grid=(m // tm,),
        in_specs=[
            pl.BlockSpec((tm, 1), lambda i: (i, 0)),
            pl.BlockSpec((1, V7X_LANES), lambda i: (0, 0)),
            pl.BlockSpec((tm, d), lambda i: (i, 0)),
            pl.BlockSpec((d, n), lambda i: (0, 0)),
        ],
        out_specs=out_specs,
        out_shape=[units(m)] * 9 + [jax.ShapeDtypeStruct((m, d), BF16)],
        scratch_shapes=[pltpu.VMEM((n_slabs, tm, HEAD_DIM), F32)],
        compiler_params=_compiler_params(("parallel",), vmem),
        name="qkv_proj",
    )(pos, invf, x, w)


def _attn_kernel(q1_ref, q4_ref, q16_ref, k1c_ref, k4c_ref, k16c_ref, k1p_ref, k4p_ref, k16p_ref,
                 v1c_ref, v4c_ref, v16c_ref, v1p_ref, v4p_ref, v16p_ref, o_ref, acc_ref, lse_ref):
    first_tile = pl.program_id(2) == 0
    blk = ATTN_SPAN
    nu = ATTN_UNITS
    scale = HEAD_DIM ** -0.5
    qi = lax.broadcasted_iota(jnp.int32, (1, blk, blk), 1)
    ki = lax.broadcasted_iota(jnp.int32, (1, blk, blk), 2)
    unit = lax.broadcasted_iota(jnp.int32, (nu, 1, 1), 0)
    neg_inf = -jnp.inf
    qk = lambda a, b: jnp.einsum("uqd,ukd->uqk", a, b, preferred_element_type=F32)
    pv = lambda a, b: jnp.einsum("uqk,ukd->uqd", a, b, preferred_element_type=F32)

    def with_prev(fn, lhs, prev_ref, cur_ref, d):
        head = fn(lhs[:d], prev_ref[...])
        if d == nu:
            return head
        return jnp.concatenate([head, fn(lhs[d:], cur_ref[0:nu - d])], axis=0)

    refs = ((q1_ref, k1c_ref, k1p_ref, v1c_ref, v1p_ref), (q4_ref, k4c_ref, k4p_ref, v4c_ref, v4p_ref),
            (q16_ref, k16c_ref, k16p_ref, v16c_ref, v16p_ref))
    for p, (d, (q_ref, kc_ref, kp_ref, vc_ref, vp_ref)) in enumerate(zip(DILATIONS, refs)):
        q = q_ref[...]
        sp = with_prev(qk, q, kp_ref, kc_ref, d) * scale
        sc = qk(q, kc_ref[...]) * scale
        sp = jnp.where(ki >= qi, sp, neg_inf)
        sp = jnp.where(jnp.logical_and(first_tile, unit < d), neg_inf, sp)
        sc = jnp.where(ki <= qi, sc, neg_inf)
        mx = jnp.max(jnp.maximum(sp, sc), axis=-1, keepdims=True)
        ep = jnp.exp(sp - mx)
        ec = jnp.exp(sc - mx)
        den = jnp.sum(ep + ec, axis=-1, keepdims=True)
        out = (with_prev(pv, ep.astype(BF16), vp_ref, vc_ref, d) + pv(ec.astype(BF16), vc_ref[...])) / den
        lse = jnp.broadcast_to(mx + jnp.log(den), out.shape)
        if d == 1:
            acc_ref[p] = out.reshape(ATTN_TILE, HEAD_DIM)
            lse_ref[p] = lse.reshape(ATTN_TILE, HEAD_DIM)
        else:
            for u in range(nu):
                jb, r = divmod(u, d) if d < nu else (0, u)
                rows = pl.ds(d * blk * jb + r, blk, stride=d)
                acc_ref[p, rows, :] = out[u]
                lse_ref[p, rows, :] = lse[u]

    lse = lse_ref[...]
    w = jnp.exp(lse - jnp.max(lse, axis=0, keepdims=True))
    merged = jnp.sum(w * acc_ref[...], axis=0) / jnp.sum(w, axis=0)
    o_ref[...] = merged.astype(o_ref.dtype)


def _attention(qkv, *, batch, seq):
    nt = seq // ATTN_TILE
    heads = N_KV_HEADS
    nu = ATTN_UNITS
    q1, q4, q16, k1, k4, k16, v1, v4, v16 = qkv

    def cur():
        return pl.BlockSpec((nu, None, ATTN_SPAN, HEAD_DIM), lambda b, h, n: (b * nt + n, h, 0, 0))

    def prev(d):
        per = nu // d
        return pl.BlockSpec((d, None, ATTN_SPAN, HEAD_DIM),
                            lambda b, h, n: (jnp.maximum((b * nt + n) * per - 1, 0), h, 0, 0))

    unit_bytes = ATTN_SPAN * HEAD_DIM * 2
    tile_f32 = ATTN_TILE * HEAD_DIM * 4
    vmem = 2 * (9 * nu + 2 * sum(DILATIONS)) * unit_bytes + 2 * tile_f32 // 2 + 6 * tile_f32 + 12 * tile_f32 + (4 << 20)
    return pl.pallas_call(
        _attn_kernel,
        grid=(batch, heads, nt),
        in_specs=[cur(), cur(), cur(), cur(), cur(), cur(), prev(1), prev(4), prev(16),
                  cur(), cur(), cur(), prev(1), prev(4), prev(16)],
        out_specs=pl.BlockSpec((ATTN_TILE, HEAD_DIM), lambda b, h, n: (b * nt + n, h)),
        out_shape=jax.ShapeDtypeStruct((batch * seq, heads * HEAD_DIM), BF16),
        scratch_shapes=[pltpu.VMEM((len(DILATIONS), ATTN_TILE, HEAD_DIM), F32)] * 2,
        compiler_params=_compiler_params(("parallel", "parallel", "arbitrary"), vmem),
        name="dilated_attention",
    )(q1, q4, q16, k1, k4, k16, k1, k4, k16, v1, v4, v16, v1, v4, v16)


def _linear_scan_rows(a, u, h0):
    rows, width = a.shape
    sub = V7X_SUBLANES
    groups = rows // sub
    a = a.reshape(groups, sub, width)
    u = u.reshape(groups, sub, width)
    row = lax.broadcasted_iota(jnp.int32, a.shape, 1)
    d = 1
    while d < sub:
        keep = row >= d
        u = u + a * jnp.where(keep, pltpu.roll(u, d, 1), 0.0)
        a = a * jnp.where(keep, pltpu.roll(a, d, 1), 1.0)
        d *= 2
    out = []
    h = h0
    for g in range(groups):
        hg = u[g] + a[g] * h
        out.append(hg)
        h = hg[sub - 1:sub, :]
    return jnp.concatenate(out, axis=0), h


def _gelu_tanh(y):
    z = 0.7978845608028654 * (y + 0.044715 * (y * y * y))
    return y * jax.nn.sigmoid(2.0 * z)


def _lru_kernel(hin_ref, wx_ref, wy_ref, cw_ref, cb_ref, wri_ref, br_ref, bi_ref, lam_ref, o_ref,
                wxb_ref, wyb_ref, xs_ref, ys_ref, h_ref):
    tl = hin_ref.shape[0]
    pad = V7X_SUBLANES
    s = pl.program_id(2)

    @pl.when(s == 0)
    def _():
        wxb_ref[...] = wx_ref[...].astype(BF16)
        wyb_ref[...] = wy_ref[...].astype(BF16)
        xs_ref[...] = jnp.zeros_like(xs_ref)
        ys_ref[...] = jnp.zeros_like(ys_ref)

    @pl.when(s <= 1)
    def _():
        xs_ref[0:pad, :] = jnp.zeros((pad, xs_ref.shape[1]), F32)
        h_ref[...] = jnp.zeros_like(h_ref)

    half = wxb_ref.shape[1] // 2

    def project(dst_ref, rows, w_ref, part):
        cols = slice(part * half, (part + 1) * half)
        dst_ref[rows, cols] = jnp.dot(hin_ref[...], w_ref[:, cols], preferred_element_type=F32)

    x = xs_ref[pad:pad + tl, :]
    y = ys_ref[...]
    project(ys_ref, slice(0, tl), wyb_ref, 0)
    xc = cw_ref[CONV_WIDTH - 1:CONV_WIDTH, :] * x
    for j in range(CONV_WIDTH - 1):
        shift = CONV_WIDTH - 1 - j
        xc = xc + cw_ref[j:j + 1, :] * xs_ref[pad - shift:pad - shift + tl, :]
    xc = xc + cb_ref[...]
    xs_ref[0:pad, :] = x[tl - pad:tl]

    bw = LRU_BLOCK_WIDTH
    pre = [jnp.dot(xc[:, g * bw:(g + 1) * bw].astype(BF16), wri_ref[g], preferred_element_type=F32)
           for g in range(x.shape[1] // bw)]
    project(ys_ref, slice(0, tl), wyb_ref, 1)
    project(xs_ref, slice(pad, pad + tl), wxb_ref, 0)

    r = jax.nn.sigmoid(jnp.concatenate([t[:, :bw] for t in pre], axis=1) + br_ref[...])
    i = jax.nn.sigmoid(jnp.concatenate([t[:, bw:] for t in pre], axis=1) + bi_ref[...])
    neg_lam = -lam_ref[...]
    softplus = jnp.maximum(neg_lam, 0.0) + jnp.log1p(jnp.exp(-jnp.abs(neg_lam)))
    log_a = (-LRU_C * softplus) * r
    a = jnp.exp(log_a)
    one_m_a2 = 1.0 - a * a
    root = jnp.where(one_m_a2 > 0.0, one_m_a2 * lax.rsqrt(one_m_a2), 0.0)
    u = root * (i * xc)
    project(xs_ref, slice(pad, pad + tl), wxb_ref, 1)
    h, h_ref[...] = _linear_scan_rows(a, u, h_ref[...])
    o_ref[...] = (h * _gelu_tanh(y)).astype(o_ref.dtype)


def _rg_lru(hin, w_in, conv_w, conv_b, w_ri, b_r, b_i, lam, *, batch, seq, x_col, width, tl=256, tc=512):
    m, d = hin.shape
    nt = seq // tl
    nc = width // tc
    assert seq % tl == 0 and width % tc == 0 and x_col % tc == 0
    gpc = tc // LRU_BLOCK_WIDTH
    row = pl.BlockSpec((1, tc), lambda b, c, n: (0, c))
    vmem = (2 * tl * d * 2 + 2 * 2 * d * tc * 4 + 2 * d * tc * 2 + 2 * tl * tc * 2
            + 2 * gpc * 128 * 256 * 2 + 16 * tl * tc * 4 + (4 << 20))
    return pl.pallas_call(
        _lru_kernel,
        grid=(batch, nc, nt + 1),
        in_specs=[
            pl.BlockSpec((tl, d), lambda b, c, n: (b * nt + jnp.minimum(n, nt - 1), 0)),
            pl.BlockSpec((d, tc), lambda b, c, n: (0, x_col // tc + c)),
            pl.BlockSpec((d, tc), lambda b, c, n: (0, x_col // tc + nc + c)),
            pl.BlockSpec((CONV_WIDTH, tc), lambda b, c, n: (0, c)),
            row,
            pl.BlockSpec((gpc, LRU_BLOCK_WIDTH, 2 * LRU_BLOCK_WIDTH), lambda b, c, n: (c, 0, 0)),
            row, row, row,
        ],
        out_specs=pl.BlockSpec((tl, tc), lambda b, c, n: (b * nt + jnp.maximum(n - 1, 0), c)),
        out_shape=jax.ShapeDtypeStruct((m, width), BF16),
        scratch_shapes=[pltpu.VMEM((d, tc), BF16), pltpu.VMEM((d, tc), BF16),
                        pltpu.VMEM((tl + V7X_SUBLANES, tc), F32), pltpu.VMEM((tl, tc), F32),
                        pltpu.VMEM((1, tc), F32)],
        compiler_params=_compiler_params(("parallel", "parallel", "arbitrary"), vmem),
        name="rg_lru",
    )(hin, w_in, w_in, conv_w, conv_b, w_ri, b_r, b_i, lam)


def _out_ln_kernel(x_ref, a_ref, r_ref, wa_ref, wr_ref, g_ref, b_ref, o_ref, *, alpha, row_chunks):
    rows = x_ref.shape[0] // row_chunks
    for c in range(row_chunks):
        sl = slice(c * rows, (c + 1) * rows)
        mix = (jnp.dot(a_ref[sl, :], wa_ref[...], preferred_element_type=F32)
               + jnp.dot(r_ref[sl, :], wr_ref[...], preferred_element_type=F32))
        o_ref[sl, :] = _layer_norm_rows(alpha * x_ref[sl, :] + mix, g_ref[...], b_ref[...])


def _out_ln(x, attn, rec, wa, wr, g, b, *, alpha, tm=512, row_chunks=2):
    m, d = x.shape
    ka, kr = attn.shape[1], rec.shape[1]
    assert m % tm == 0 and tm % row_chunks == 0
    vmem = (2 * 2 * tm * d * 4 + 2 * tm * (ka + kr) * 2 + 2 * (ka + kr) * d * 2 + 3 * tm * d * 4 + (4 << 20))
    full = lambda shape: pl.BlockSpec(shape, lambda i: (0, 0))
    return pl.pallas_call(
        functools.partial(_out_ln_kernel, alpha=alpha, row_chunks=row_chunks),
        grid=(m // tm,),
        in_specs=[
            pl.BlockSpec((tm, d), lambda i: (i, 0)),
            pl.BlockSpec((tm, ka), lambda i: (i, 0)),
            pl.BlockSpec((tm, kr), lambda i: (i, 0)),
            full((ka, d)), full((kr, d)), full((1, d)), full((1, d)),
        ],
        out_specs=pl.BlockSpec((tm, d), lambda i: (i, 0)),
        out_shape=jax.ShapeDtypeStruct((m, d), F32),
        compiler_params=_compiler_params(("parallel",), vmem),
        name="out_ln",
    )(x, attn, rec, wa, wr, g, b)


def _ple_kernel(x_ref, p_ref, wg_ref, wp_ref, o_ref):
    x = x_ref[...]
    gate = jax.nn.sigmoid(jnp.dot(x.astype(BF16), wg_ref[...], preferred_element_type=F32))
    emb = jnp.dot(p_ref[...].astype(BF16), wp_ref[...], preferred_element_type=F32)
    o_ref[...] = x + gate * emb


def _ple(x, p, wg, wp, *, tm=512):
    m, d = x.shape
    dp = p.shape[1]
    assert m % tm == 0
    vmem = 2 * 2 * tm * d * 4 + 2 * tm * dp * 4 + 2 * (d + dp) * d * 2 + 3 * tm * d * 4 + (4 << 20)
    full = lambda shape: pl.BlockSpec(shape, lambda i: (0, 0))
    return pl.pallas_call(
        _ple_kernel,
        grid=(m // tm,),
        in_specs=[
            pl.BlockSpec((tm, d), lambda i: (i, 0)),
            pl.BlockSpec((tm, dp), lambda i: (i, 0)),
            full((d, d)), full((dp, d)),
        ],
        out_specs=pl.BlockSpec((tm, d), lambda i: (i, 0)),
        out_shape=jax.ShapeDtypeStruct((m, d), F32),
        compiler_params=_compiler_params(("parallel",), vmem),
        name="ple",
    )(x, p, wg, wp)


def kernel(x, p, positions, ffn1_w_gate, ffn1_w_up, ffn1_w_down, ln1_g, ln1_b, w_in, conv_w, conv_b, w_rgate, b_rgate, w_igate, b_igate, lru_lambda, w_out, ln2_g, ln2_b, ffn2_w_gate, ffn2_w_up, ffn2_w_down, ln3_g, ln3_b, w_ple_proj, w_ple_gate):
    batch, seq, d_model = x.shape
    depth = ffn1_w_gate.shape[0]
    m = batch * seq
    kv_width = N_KV_HEADS * HEAD_DIM
    qkv_width = (len(DILATIONS) + 2) * kv_width
    lru_width = d_model - kv_width
    alpha = (2.0 * depth) ** 0.25
    assert w_in.shape[2] == qkv_width + 2 * lru_width

    half = ROT_DIMS // 2
    inv_freq = jnp.power(jnp.float32(ROPE_THETA), -jnp.arange(half, dtype=F32) * (2.0 / ROT_DIMS))
    invf = jnp.zeros((1, V7X_LANES), F32).at[0, :ROT_DIMS].set(jnp.tile(inv_freq, 2))
    pos = positions.reshape(m, 1)
    row = lambda v: v.reshape(1, -1)

    h = x.reshape(m, d_model)
    for i in range(depth):
        h = _ffn_ln(h, ffn1_w_gate[i].astype(BF16), ffn1_w_up[i].astype(BF16), ffn1_w_down[i].astype(BF16),
                    row(ln1_g[i]), row(ln1_b[i]), alpha=alpha)
        *qkv, hb = _qkv_proj(h, pos, invf, w_in[i, :, :qkv_width].astype(BF16))
        attn = _attention(qkv, batch=batch, seq=seq)
        w_ri = jnp.concatenate([w_rgate[i], w_igate[i]], axis=-1).astype(BF16)
        rec = _rg_lru(hb, w_in[i], conv_w[i], row(conv_b[i]), w_ri, row(b_rgate[i]), row(b_igate[i]),
                      row(lru_lambda[i]), batch=batch, seq=seq, x_col=qkv_width, width=lru_width)
        wo = w_out[i].astype(BF16)
        h = _out_ln(h, attn, rec, wo[:kv_width], wo[kv_width:], row(ln2_g[i]), row(ln2_b[i]), alpha=alpha)
        h = _ffn_ln(h, ffn2_w_gate[i].astype(BF16), ffn2_w_up[i].astype(BF16), ffn2_w_down[i].astype(BF16),
                    row(ln3_g[i]), row(ln3_b[i]), alpha=alpha)
        h = _ple(h, p[i].reshape(m, -1), w_ple_gate[i].astype(BF16), w_ple_proj[i].astype(BF16))
    return h.reshape(batch, seq, d_model)
```

```python
import functools

import jax
import jax.numpy as jnp
from jax import lax
from jax.experimental import pallas as pl
from jax.experimental.pallas import tpu as pltpu

F32 = jnp.float32
BF16 = jnp.bfloat16

HEAD_DIM = 128
N_KV_HEADS = 4
DILATIONS = (1, 4, 16)
ATTN_SPAN = 128
ROT_DIMS = HEAD_DIM // 4
ROPE_THETA = 500000.0
LRU_BLOCK_WIDTH = 128
CONV_WIDTH = 4
LRU_C = 8.0
LN_EPS = 1e-5

V7X_LANES = 128
V7X_SUBLANES = 8
V7X_VMEM_SCOPED_BYTES = 60000 * 1024

ATTN_TILE = ATTN_SPAN * DILATIONS[-1]
ATTN_UNITS = ATTN_TILE // ATTN_SPAN
QKV_ROWS = ATTN_SPAN * DILATIONS[1]


def _compiler_params(semantics, vmem_estimate_bytes):
    limit = min(int(vmem_estimate_bytes), V7X_VMEM_SCOPED_BYTES)
    return pltpu.CompilerParams(dimension_semantics=semantics, vmem_limit_bytes=limit)


def _layer_norm_rows(y, g, b):
    mu = jnp.mean(y, axis=-1, keepdims=True)
    yc = y - mu
    var = jnp.mean(yc * yc, axis=-1, keepdims=True)
    return yc * lax.rsqrt(var + LN_EPS) * g + b


def _ffn_ln_kernel(x_ref, wg_ref, wu_ref, wd_ref, g_ref, b_ref, o_ref, *, alpha, ln_row_chunks):
    f = pl.program_id(1)
    last = pl.num_programs(1) - 1

    def partial_sum(rows):
        xb = x_ref[rows, :].astype(BF16)
        gate = jnp.dot(xb, wg_ref[...], preferred_element_type=F32)
        up = jnp.dot(xb, wu_ref[...], preferred_element_type=F32)
        act = (jax.nn.silu(gate) * up).astype(BF16)
        return jnp.dot(act, wd_ref[...], preferred_element_type=F32)

    def accumulated(rows):
        return o_ref[rows, :] + partial_sum(rows)

    @pl.when(f == 0)
    def _():
        o_ref[...] = partial_sum(slice(None))

    @pl.when(jnp.logical_and(f > 0, f < last))
    def _():
        o_ref[...] = accumulated(slice(None))

    @pl.when(f == last)
    def _():
        rows_per = x_ref.shape[0] // ln_row_chunks
        for c in range(ln_row_chunks):
            rows = slice(c * rows_per, (c + 1) * rows_per)
            y = alpha * x_ref[rows, :] + 0.5 * accumulated(rows)
            o_ref[rows, :] = _layer_norm_rows(y, g_ref[...], b_ref[...])


def _ffn_ln(x, wg, wu, wd, g, b, *, alpha, tm=1024, tf=512, ln_row_chunks=4):
    m, d = x.shape
    dff = wg.shape[1]
    assert m % tm == 0 and dff % tf == 0
    vmem = (2 * tm * d * (4 + 4) + tm * d * 2 + 2 * 3 * d * tf * 2
            + 3 * tm * tf * 4 + (6 << 20))
    return pl.pallas_call(
        functools.partial(_ffn_ln_kernel, alpha=alpha, ln_row_chunks=ln_row_chunks),
        grid=(m // tm, dff // tf),
        in_specs=[
            pl.BlockSpec((tm, d), lambda i, f: (i, 0)),
            pl.BlockSpec((d, tf), lambda i, f: (0, f)),
            pl.BlockSpec((d, tf), lambda i, f: (0, f)),
            pl.BlockSpec((tf, d), lambda i, f: (f, 0)),
            pl.BlockSpec((1, d), lambda i, f: (0, 0)),
            pl.BlockSpec((1, d), lambda i, f: (0, 0)),
        ],
        out_specs=pl.BlockSpec((tm, d), lambda i, f: (i, 0)),
        out_shape=jax.ShapeDtypeStruct((m, d), F32),
        compiler_params=_compiler_params(("parallel", "arbitrary"), vmem),
        name="ffn_ln",
    )(x, wg, wu, wd, g, b)


def _qkv_proj_kernel(pos_ref, invf_ref, x_ref, w_ref,
                     q1_ref, q4_ref, q16_ref, k1_ref, k4_ref, k16_ref, v1_ref, v4_ref, v16_ref, xb_ref,
                     slab_ref):
    tm = x_ref.shape[0]
    heads = N_KV_HEADS
    half = ROT_DIMS // 2
    kvw = heads * HEAD_DIM
    x = x_ref[...].astype(BF16)
    xb_ref[...] = x

    ang = pos_ref[...].astype(F32) * invf_ref[...]
    lane = lax.broadcasted_iota(jnp.int32, ang.shape, 1)
    sin = jnp.sin(ang)
    cos = jnp.cos(ang)
    sin_lo = jnp.where(lane < half, -sin, 0.0)
    sin_hi = jnp.where(lane >= half, sin, 0.0)

    def rotate(t):
        return t * cos + pltpu.roll(t, HEAD_DIM - half, 1) * sin_lo + pltpu.roll(t, half, 1) * sin_hi

    def write_units(t, h, slab, o1_ref, o4_ref, o16_ref):
        if o1_ref is not None:
            for jb in range(tm // ATTN_SPAN):
                o1_ref[jb, h] = t[jb * ATTN_SPAN:(jb + 1) * ATTN_SPAN].astype(BF16)
        if o4_ref is None and o16_ref is None:
            return
        slab_ref[slab] = t
        for o_ref, d in ((o4_ref, DILATIONS[1]), (o16_ref, DILATIONS[2])):
            if o_ref is not None:
                for r in range(d):
                    o_ref[r, h] = slab_ref[slab, pl.ds(r, tm // d, stride=d), :].astype(BF16)

    groups = (
        (0, True, (q1_ref, None, None)), (1, True, (None, q4_ref, None)), (2, True, (None, None, q16_ref)),
        (3, True, (k1_ref, k4_ref, k16_ref)), (4, False, (v1_ref, v4_ref, v16_ref)),
    )
    for step, (j, rot, outs) in enumerate(groups):
        acc = jnp.dot(x, w_ref[:, j * kvw:(j + 1) * kvw], preferred_element_type=F32)
        for h in range(heads):
            t = acc[:, h * HEAD_DIM:(h + 1) * HEAD_DIM]
            write_units(rotate(t) if rot else t, h, (step * heads + h) % slab_ref.shape[0], *outs)


def _qkv_proj(x, pos, invf, w):
    m, d = x.shape
    tm = QKV_ROWS
    heads = N_KV_HEADS
    n = w.shape[1]
    per_tile = ATTN_TILE // tm
    assert m % ATTN_TILE == 0 and n == 5 * heads * HEAD_DIM
    units = lambda rows: jax.ShapeDtypeStruct((rows // ATTN_SPAN, heads, ATTN_SPAN, HEAD_DIM), BF16)
    flat_spec = pl.BlockSpec((tm // ATTN_SPAN, heads, ATTN_SPAN, HEAD_DIM), lambda i: (i, 0, 0, 0))
    wide_spec = pl.BlockSpec((DILATIONS[2], heads, tm // DILATIONS[2], HEAD_DIM),
                             lambda i: (i // per_tile, 0, i % per_tile, 0))
    out_specs = [flat_spec, flat_spec, wide_spec] * 3 + [pl.BlockSpec((tm, d), lambda i: (i, 0))]
    n_slabs = 4
    vmem = (2 * tm * d * 4 + tm * d * 2 + 2 * d * n * 2 + 2 * 9 * tm * heads * HEAD_DIM * 2
            + n_slabs * tm * HEAD_DIM * 4 + 8 * tm * heads * HEAD_DIM * 4 + (6 << 20))
    return pl.pallas_call(
        _qkv_proj_kernel,
        grid=(m // tm,),
        in_specs=[
            pl.BlockSpec((tm, 1), lambda i: (i, 0)),
            pl.BlockSpec((1, V7X_LANES), lambda i: (0, 0)),
            pl.BlockSpec((tm, d), lambda i: (i, 0)),
            pl.BlockSpec((d, n), lambda i: (0, 0)),
        ],
        out_specs=out_specs,
        out_shape=[units(m)] * 9 + [jax.ShapeDtypeStruct((m, d), BF16)],
        scratch_shapes=[pltpu.VMEM((n_slabs, tm, HEAD_DIM), F32)],
        compiler_params=_compiler_params(("parallel",), vmem),
        name="qkv_proj",
    )(pos, invf, x, w)


def _attn_kernel(q1_ref, q4_ref, q16_ref, k1c_ref, k4c_ref, k16c_ref, k1p_ref, k4p_ref, k16p_ref,
                 v1c_ref, v4c_ref, v16c_ref, v1p_ref, v4p_ref, v16p_ref, o_ref, acc_ref, lse_ref):
    first_tile = pl.program_id(2) == 0
    blk = ATTN_SPAN
    nu = ATTN_UNITS
    scale = HEAD_DIM ** -0.5
    qi = lax.broadcasted_iota(jnp.int32, (1, blk, blk), 1)
    ki = lax.broadcasted_iota(jnp.int32, (1, blk, blk), 2)
    unit = lax.broadcasted_iota(jnp.int32, (nu, 1, 1), 0)
    neg_inf = -jnp.inf
    qk = lambda a, b: jnp.einsum("uqd,ukd->uqk", a, b, preferred_element_type=F32)
    pv = lambda a, b: jnp.einsum("uqk,ukd->uqd", a, b, preferred_element_type=F32)

    def with_prev(fn, lhs, prev_ref, cur_ref, d):
        head = fn(lhs[:d], prev_ref[...])
        if d == nu:
            return head
        return jnp.concatenate([head, fn(lhs[d:], cur_ref[0:nu - d])], axis=0)

    refs = ((q1_ref, k1c_ref, k1p_ref, v1c_ref, v1p_ref), (q4_ref, k4c_ref, k4p_ref, v4c_ref, v4p_ref),
            (q16_ref, k16c_ref, k16p_ref, v16c_ref, v16p_ref))
    for p, (d, (q_ref, kc_ref, kp_ref, vc_ref, vp_ref)) in enumerate(zip(DILATIONS, refs)):
        q = q_ref[...]
        sp = with_prev(qk, q, kp_ref, kc_ref, d) * scale
        sc = qk(q, kc_ref[...]) * scale
        sp = jnp.where(ki >= qi, sp, neg_inf)
        sp = jnp.where(jnp.logical_and(first_tile, unit < d), neg_inf, sp)
        sc = jnp.where(ki <= qi, sc, neg_inf)
        mx = jnp.max(jnp.maximum(sp, sc), axis=-1, keepdims=True)
        ep = jnp.exp(sp - mx)
        ec = jnp.exp(sc - mx)
        den = jnp.sum(ep + ec, axis=-1, keepdims=True)
        out = (with_prev(pv, ep.astype(BF16), vp_ref, vc_ref, d) + pv(ec.astype(BF16), vc_ref[...])) / den
        lse = jnp.broadcast_to(mx + jnp.log(den), out.shape)
        if d == 1:
            acc_ref[p] = out.reshape(ATTN_TILE, HEAD_DIM)
            lse_ref[p] = lse.reshape(ATTN_TILE, HEAD_DIM)
        else:
            for u in range(nu):
                jb, r = divmod(u, d) if d < nu else (0, u)
                rows = pl.ds(d * blk * jb + r, blk, stride=d)
                acc_ref[p, rows, :] = out[u]
                lse_ref[p, rows, :] = lse[u]

    lse = lse_ref[...]
    w = jnp.exp(lse - jnp.max(lse, axis=0, keepdims=True))
    merged = jnp.sum(w * acc_ref[...], axis=0) / jnp.sum(w, axis=0)
    o_ref[...] = merged.astype(o_ref.dtype)


def _attention(qkv, *, batch, seq):
    nt = seq // ATTN_TILE
    heads = N_KV_HEADS
    nu = ATTN_UNITS
    q1, q4, q16, k1, k4, k16, v1, v4, v16 = qkv

    def cur():
        return pl.BlockSpec((nu, None, ATTN_SPAN, HEAD_DIM), lambda b, h, n: (b * nt + n, h, 0, 0))

    def prev(d):
        per = nu // d
        return pl.BlockSpec((d, None, ATTN_SPAN, HEAD_DIM),
                            lambda b, h, n: (jnp.maximum((b * nt + n) * per - 1, 0), h, 0, 0))

    unit_bytes = ATTN_SPAN * HEAD_DIM * 2
    tile_f32 = ATTN_TILE * HEAD_DIM * 4
    vmem = 2 * (9 * nu + 2 * sum(DILATIONS)) * unit_bytes + 2 * tile_f32 // 2 + 6 * tile_f32 + 12 * tile_f32 + (4 << 20)
    return pl.pallas_call(
        _attn_kernel,
        grid=(batch, heads, nt),
        in_specs=[cur(), cur(), cur(), cur(), cur(), cur(), prev(1), prev(4), prev(16),
                  cur(), cur(), cur(), prev(1), prev(4), prev(16)],
        out_specs=pl.BlockSpec((ATTN_TILE, HEAD_DIM), lambda b, h, n: (b * nt + n, h)),
        out_shape=jax.ShapeDtypeStruct((batch * seq, heads * HEAD_DIM), BF16),
        scratch_shapes=[pltpu.VMEM((len(DILATIONS), ATTN_TILE, HEAD_DIM), F32)] * 2,
        compiler_params=_compiler_params(("parallel", "parallel", "arbitrary"), vmem),
        name="dilated_attention",
    )(q1, q4, q16, k1, k4, k16, k1, k4, k16, v1, v4, v16, v1, v4, v16)


SCAN_PAD_ROWS = V7X_SUBLANES


def _linear_scan_rows(a, u, h0, sa_ref, su_ref):
    rows, width = a.shape
    sub = V7X_SUBLANES
    run = rows // sub
    pitch = run + SCAN_PAD_ROWS
    lanes = V7X_LANES
    out = []
    carry = []
    for c in range(width // lanes):
        cols = slice(c * lanes, (c + 1) * lanes)
        for s in range(sub):
            sa_ref[c, s * pitch:s * pitch + run, :] = a[s * run:(s + 1) * run, cols]
            su_ref[c, s * pitch:s * pitch + run, :] = u[s * run:(s + 1) * run, cols]
        h = jnp.zeros((sub, lanes), F32)
        prod = jnp.ones((sub, lanes), F32)
        hs, prods = [], []
        for k in range(run):
            ak = sa_ref[c, pl.ds(k, sub, stride=pitch), :]
            uk = su_ref[c, pl.ds(k, sub, stride=pitch), :]
            h = ak * h + uk
            prod = ak * prod
            hs.append(h)
            prods.append(prod)
        start = h0[:, cols]
        enter = [start]
        for s in range(sub - 1):
            start = h[s:s + 1, :] + prod[s:s + 1, :] * start
            enter.append(start)
        carry.append(h[sub - 1:sub, :] + prod[sub - 1:sub, :] * start)
        enter = jnp.concatenate(enter, axis=0)
        for k in range(run):
            su_ref[c, pl.ds(k, sub, stride=pitch), :] = hs[k] + prods[k] * enter
        out.append(jnp.concatenate([su_ref[c, s * pitch:s * pitch + run, :] for s in range(sub)], axis=0))
    return jnp.concatenate(out, axis=1), jnp.concatenate(carry, axis=1)


def _gelu_tanh(y):
    z = 0.7978845608028654 * (y + 0.044715 * (y * y * y))
    return y * jax.nn.sigmoid(2.0 * z)


def _lru_kernel(hin_ref, wx_ref, wy_ref, cw_ref, cb_ref, wri_ref, br_ref, bi_ref, lam_ref, o_ref,
                wxb_ref, wyb_ref, xs_ref, ys_ref, h_ref, sa_ref, su_ref):
    tl = hin_ref.shape[0]
    pad = V7X_SUBLANES
    s = pl.program_id(2)

    @pl.when(s == 0)
    def _():
        wxb_ref[...] = wx_ref[...].astype(BF16)
        wyb_ref[...] = wy_ref[...].astype(BF16)
        xs_ref[...] = jnp.zeros_like(xs_ref)
        ys_ref[...] = jnp.zeros_like(ys_ref)

    @pl.when(s <= 1)
    def _():
        xs_ref[0:pad, :] = jnp.zeros((pad, xs_ref.shape[1]), F32)
        h_ref[...] = jnp.zeros_like(h_ref)

    half = wxb_ref.shape[1] // 2

    def project(dst_ref, rows, w_ref, part):
        cols = slice(part * half, (part + 1) * half)
        dst_ref[rows, cols] = jnp.dot(hin_ref[...], w_ref[:, cols], preferred_element_type=F32)

    x = xs_ref[pad:pad + tl, :]
    y = ys_ref[...]
    project(ys_ref, slice(0, tl), wyb_ref, 0)
    xc = cw_ref[CONV_WIDTH - 1:CONV_WIDTH, :] * x
    for j in range(CONV_WIDTH - 1):
        shift = CONV_WIDTH - 1 - j
        xc = xc + cw_ref[j:j + 1, :] * xs_ref[pad - shift:pad - shift + tl, :]
    xc = xc + cb_ref[...]
    xs_ref[0:pad, :] = x[tl - pad:tl]

    bw = LRU_BLOCK_WIDTH
    pre = [jnp.dot(xc[:, g * bw:(g + 1) * bw].astype(BF16), wri_ref[g], preferred_element_type=F32)
           for g in range(x.shape[1] // bw)]
    project(ys_ref, slice(0, tl), wyb_ref, 1)
    project(xs_ref, slice(pad, pad + tl), wxb_ref, 0)

    r = jax.nn.sigmoid(jnp.concatenate([t[:, :bw] for t in pre], axis=1) + br_ref[...])
    i = jax.nn.sigmoid(jnp.concatenate([t[:, bw:] for t in pre], axis=1) + bi_ref[...])
    neg_lam = -lam_ref[...]
    softplus = jnp.maximum(neg_lam, 0.0) + jnp.log1p(jnp.exp(-jnp.abs(neg_lam)))
    log_a = (-LRU_C * softplus) * r
    a = jnp.exp(log_a)
    one_m_a2 = 1.0 - a * a
    root = jnp.where(one_m_a2 > 0.0, one_m_a2 * lax.rsqrt(one_m_a2), 0.0)
    u = root * (i * xc)
    project(xs_ref, slice(pad, pad + tl), wxb_ref, 1)
    h, h_ref[...] = _linear_scan_rows(a, u, h_ref[...], sa_ref, su_ref)
    o_ref[...] = (h * _gelu_tanh(y)).astype(o_ref.dtype)


def _rg_lru(hin, w_in, conv_w, conv_b, w_ri, b_r, b_i, lam, *, batch, seq, x_col, width, tl=256, tc=512):
    m, d = hin.shape
    nt = seq // tl
    nc = width // tc
    assert seq % tl == 0 and width % tc == 0 and x_col % tc == 0
    gpc = tc // LRU_BLOCK_WIDTH
    row = pl.BlockSpec((1, tc), lambda b, c, n: (0, c))
    vmem = (2 * tl * d * 2 + 2 * 2 * d * tc * 4 + 2 * d * tc * 2 + 2 * tl * tc * 2
            + 2 * gpc * 128 * 256 * 2 + 16 * tl * tc * 4 + (4 << 20))
    return pl.pallas_call(
        _lru_kernel,
        grid=(batch, nc, nt + 1),
        in_specs=[
            pl.BlockSpec((tl, d), lambda b, c, n: (b * nt + jnp.minimum(n, nt - 1), 0)),
            pl.BlockSpec((d, tc), lambda b, c, n: (0, x_col // tc + c)),
            pl.BlockSpec((d, tc), lambda b, c, n: (0, x_col // tc + nc + c)),
            pl.BlockSpec((CONV_WIDTH, tc), lambda b, c, n: (0, c)),
            row,
            pl.BlockSpec((gpc, LRU_BLOCK_WIDTH, 2 * LRU_BLOCK_WIDTH), lambda b, c, n: (c, 0, 0)),
            row, row, row,
        ],
        out_specs=pl.BlockSpec((tl, tc), lambda b, c, n: (b * nt + jnp.maximum(n - 1, 0), c)),
        out_shape=jax.ShapeDtypeStruct((m, width), BF16),
        scratch_shapes=[pltpu.VMEM((d, tc), BF16), pltpu.VMEM((d, tc), BF16),
                        pltpu.VMEM((tl + V7X_SUBLANES, tc), F32), pltpu.VMEM((tl, tc), F32),
                        pltpu.VMEM((1, tc), F32)]
        + [pltpu.VMEM((tc // V7X_LANES, tl + V7X_SUBLANES * SCAN_PAD_ROWS, V7X_LANES), F32)] * 2,
        compiler_params=_compiler_params(("parallel", "parallel", "arbitrary"), vmem),
        name="rg_lru",
    )(hin, w_in, w_in, conv_w, conv_b, w_ri, b_r, b_i, lam)


def _resident(shape):
    return pl.BlockSpec(shape, lambda i: (0,) * len(shape), pipeline_mode=pl.Buffered(1))


def _out_ln_kernel(x_ref, a_ref, r_ref, w_ref, g_ref, b_ref, o_ref, wb_ref, *, alpha, row_chunks):
    @pl.when(pl.program_id(0) == 0)
    def _():
        wb_ref[...] = w_ref[...].astype(BF16)

    rows = x_ref.shape[0] // row_chunks
    ka = a_ref.shape[1]
    for c in range(row_chunks):
        sl = slice(c * rows, (c + 1) * rows)
        mix = (jnp.dot(a_ref[sl, :], wb_ref[:ka, :], preferred_element_type=F32)
               + jnp.dot(r_ref[sl, :], wb_ref[ka:, :], preferred_element_type=F32))
        o_ref[sl, :] = _layer_norm_rows(alpha * x_ref[sl, :] + mix, g_ref[...], b_ref[...])


def _out_ln(x, attn, rec, w, g, b, *, alpha, tm=512, row_chunks=2):
    m, d = x.shape
    ka, kr = attn.shape[1], rec.shape[1]
    assert m % tm == 0 and tm % row_chunks == 0 and w.shape == (ka + kr, d)
    vmem = (2 * 2 * tm * d * 4 + 2 * tm * (ka + kr) * 2 + (ka + kr) * d * (4 + 2) + 3 * tm * d * 4 + (4 << 20))
    return pl.pallas_call(
        functools.partial(_out_ln_kernel, alpha=alpha, row_chunks=row_chunks),
        grid=(m // tm,),
        in_specs=[
            pl.BlockSpec((tm, d), lambda i: (i, 0)),
            pl.BlockSpec((tm, ka), lambda i: (i, 0)),
            pl.BlockSpec((tm, kr), lambda i: (i, 0)),
            _resident((ka + kr, d)), _resident((1, d)), _resident((1, d)),
        ],
        out_specs=pl.BlockSpec((tm, d), lambda i: (i, 0)),
        out_shape=jax.ShapeDtypeStruct((m, d), F32),
        scratch_shapes=[pltpu.VMEM((ka + kr, d), BF16)],
        compiler_params=_compiler_params(("arbitrary",), vmem),
        name="out_ln",
    )(x, attn, rec, w, g, b)


def _ple_kernel(x_ref, p_ref, wg_ref, wp_ref, o_ref, wgb_ref, wpb_ref):
    @pl.when(pl.program_id(0) == 0)
    def _():
        wgb_ref[...] = wg_ref[...].astype(BF16)
        wpb_ref[...] = wp_ref[...].astype(BF16)

    x = x_ref[...]
    gate = jax.nn.sigmoid(jnp.dot(x.astype(BF16), wgb_ref[...], preferred_element_type=F32))
    emb = jnp.dot(p_ref[...].astype(BF16), wpb_ref[...], preferred_element_type=F32)
    o_ref[...] = x + gate * emb


def _ple(x, p, wg, wp, *, tm=512):
    m, d = x.shape
    dp = p.shape[1]
    assert m % tm == 0
    vmem = 2 * 2 * tm * d * 4 + 2 * tm * dp * 4 + (d + dp) * d * (4 + 2) + 3 * tm * d * 4 + (4 << 20)
    return pl.pallas_call(
        _ple_kernel,
        grid=(m // tm,),
        in_specs=[
            pl.BlockSpec((tm, d), lambda i: (i, 0)),
            pl.BlockSpec((tm, dp), lambda i: (i, 0)),
            _resident((d, d)), _resident((dp, d)),
        ],
        out_specs=pl.BlockSpec((tm, d), lambda i: (i, 0)),
        out_shape=jax.ShapeDtypeStruct((m, d), F32),
        scratch_shapes=[pltpu.VMEM((d, d), BF16), pltpu.VMEM((dp, d), BF16)],
        compiler_params=_compiler_params(("arbitrary",), vmem),
        name="ple",
    )(x, p, wg, wp)


def kernel(x, p, positions, ffn1_w_gate, ffn1_w_up, ffn1_w_down, ln1_g, ln1_b, w_in, conv_w, conv_b, w_rgate, b_rgate, w_igate, b_igate, lru_lambda, w_out, ln2_g, ln2_b, ffn2_w_gate, ffn2_w_up, ffn2_w_down, ln3_g, ln3_b, w_ple_proj, w_ple_gate):
    batch, seq, d_model = x.shape
    depth = ffn1_w_gate.shape[0]
    m = batch * seq
    kv_width = N_KV_HEADS * HEAD_DIM
    qkv_width = (len(DILATIONS) + 2) * kv_width
    lru_width = d_model - kv_width
    alpha = (2.0 * depth) ** 0.25
    assert w_in.shape[2] == qkv_width + 2 * lru_width

    half = ROT_DIMS // 2
    inv_freq = jnp.power(jnp.float32(ROPE_THETA), -jnp.arange(half, dtype=F32) * (2.0 / ROT_DIMS))
    invf = jnp.zeros((1, V7X_LANES), F32).at[0, :ROT_DIMS].set(jnp.tile(inv_freq, 2))
    pos = positions.reshape(m, 1)
    row = lambda v: v.reshape(1, -1)

    h = x.reshape(m, d_model)
    for i in range(depth):
        h = _ffn_ln(h, ffn1_w_gate[i].astype(BF16), ffn1_w_up[i].astype(BF16), ffn1_w_down[i].astype(BF16),
                    row(ln1_g[i]), row(ln1_b[i]), alpha=alpha)
        *qkv, hb = _qkv_proj(h, pos, invf, w_in[i, :, :qkv_width].astype(BF16))
        attn = _attention(qkv, batch=batch, seq=seq)
        w_ri = jnp.concatenate([w_rgate[i], w_igate[i]], axis=-1).astype(BF16)
        rec = _rg_lru(hb, w_in[i], conv_w[i], row(conv_b[i]), w_ri, row(b_rgate[i]), row(b_igate[i]),
                      row(lru_lambda[i]), batch=batch, seq=seq, x_col=qkv_width, width=lru_width)
        h = _out_ln(h, attn, rec, w_out[i], row(ln2_g[i]), row(ln2_b[i]), alpha=alpha)
        h = _ffn_ln(h, ffn2_w_gate[i].astype(BF16), ffn2_w_up[i].astype(BF16), ffn2_w_down[i].astype(BF16),
                    row(ln3_g[i]), row(ln3_b[i]), alpha=alpha)
        h = _ple(h, p[i].reshape(m, -1), w_ple_gate[i], w_ple_proj[i])
    return h.reshape(batch, seq, d_model)
```

```python
import functools

import jax
import jax.numpy as jnp
from jax import lax
from jax.experimental import pallas as pl
from jax.experimental.pallas import tpu as pltpu

F32 = jnp.float32
BF16 = jnp.bfloat16

HEAD_DIM = 128
N_KV_HEADS = 4
DILATIONS = (1, 4, 16)
ATTN_SPAN = 128
ROT_DIMS = HEAD_DIM // 4
ROPE_THETA = 500000.0
LRU_BLOCK_WIDTH = 128
CONV_WIDTH = 4
LRU_C = 8.0
LN_EPS = 1e-5

V7X_LANES = 128
V7X_SUBLANES = 8
V7X_VMEM_SCOPED_BYTES = 60000 * 1024

ATTN_TILE = ATTN_SPAN * DILATIONS[-1]
ATTN_UNITS = ATTN_TILE // ATTN_SPAN
QKV_ROWS = ATTN_SPAN * DILATIONS[1]


def _compiler_params(semantics, vmem_estimate_bytes):
    limit = min(int(vmem_estimate_bytes), V7X_VMEM_SCOPED_BYTES)
    return pltpu.CompilerParams(dimension_semantics=semantics, vmem_limit_bytes=limit)


def _layer_norm_rows(y, g, b):
    mu = jnp.mean(y, axis=-1, keepdims=True)
    yc = y - mu
    var = jnp.mean(yc * yc, axis=-1, keepdims=True)
    return yc * lax.rsqrt(var + LN_EPS) * g + b


def _ffn_ln_kernel(x_ref, wg_ref, wu_ref, wd_ref, g_ref, b_ref, *rest, alpha, ln_row_chunks, n_cast):
    cast_in, o_ref, cast_out = rest[:n_cast], rest[n_cast], rest[n_cast + 1:]
    f = pl.program_id(1)
    last = pl.num_programs(1) - 1

    def cast_payload():
        for src, dst in zip(cast_in, cast_out):
            dst[...] = src[...].astype(BF16)

    def partial_sum(rows):
        xb = x_ref[rows, :].astype(BF16)
        gate = jnp.dot(xb, wg_ref[...], preferred_element_type=F32)
        up = jnp.dot(xb, wu_ref[...], preferred_element_type=F32)
        act = (jax.nn.silu(gate) * up).astype(BF16)
        return jnp.dot(act, wd_ref[...], preferred_element_type=F32)

    def accumulated(rows):
        return o_ref[rows, :] + partial_sum(rows)

    @pl.when(f == 0)
    def _():
        cast_payload()
        o_ref[...] = partial_sum(slice(None))

    @pl.when(jnp.logical_and(f > 0, f < last))
    def _():
        cast_payload()
        o_ref[...] = accumulated(slice(None))

    @pl.when(f == last)
    def _():
        cast_payload()
        rows_per = x_ref.shape[0] // ln_row_chunks
        for c in range(ln_row_chunks):
            rows = slice(c * rows_per, (c + 1) * rows_per)
            y = alpha * x_ref[rows, :] + 0.5 * accumulated(rows)
            o_ref[rows, :] = _layer_norm_rows(y, g_ref[...], b_ref[...])


CAST_PAYLOAD_COLS = 512


def _ffn_ln(x, wg, wu, wd, g, b, *, alpha, cast_payload=(), tm=1024, tf=512, ln_row_chunks=4):
    m, d = x.shape
    dff = wg.shape[1]
    assert m % tm == 0 and dff % tf == 0
    nf = dff // tf
    flat = [a.reshape(-1, CAST_PAYLOAD_COLS) for a in cast_payload]
    pay_specs = []
    for a in flat:
        per_tile = next(k for k in range(nf, 0, -1) if a.shape[0] % ((m // tm) * k * 2 * V7X_SUBLANES) == 0)
        block_rows = a.shape[0] // ((m // tm) * per_tile)
        pay_specs.append(pl.BlockSpec((block_rows, CAST_PAYLOAD_COLS),
                                      lambda i, f, per_tile=per_tile: (i * per_tile + jnp.minimum(f, per_tile - 1), 0)))
    vmem = (2 * tm * d * (4 + 4) + tm * d * 2 + 2 * 3 * d * tf * 2
            + 3 * tm * tf * 4 + (6 << 20)
            + sum(2 * spec.block_shape[0] * CAST_PAYLOAD_COLS * (4 + 2) for spec in pay_specs))
    out, *cast = pl.pallas_call(
        functools.partial(_ffn_ln_kernel, alpha=alpha, ln_row_chunks=ln_row_chunks, n_cast=len(flat)),
        grid=(m // tm, nf),
        in_specs=[
            pl.BlockSpec((tm, d), lambda i, f: (i, 0)),
            pl.BlockSpec((d, tf), lambda i, f: (0, f)),
            pl.BlockSpec((d, tf), lambda i, f: (0, f)),
            pl.BlockSpec((tf, d), lambda i, f: (f, 0)),
            pl.BlockSpec((1, d), lambda i, f: (0, 0)),
            pl.BlockSpec((1, d), lambda i, f: (0, 0)),
        ] + pay_specs,
        out_specs=[pl.BlockSpec((tm, d), lambda i, f: (i, 0))] + pay_specs,
        out_shape=[jax.ShapeDtypeStruct((m, d), F32)] + [jax.ShapeDtypeStruct(a.shape, BF16) for a in flat],
        compiler_params=_compiler_params(("parallel", "arbitrary"), vmem),
        name="ffn_ln",
    )(x, wg, wu, wd, g, b, *flat)
    return out, [c.reshape(a.shape) for c, a in zip(cast, cast_payload)]


def _qkv_proj_kernel(pos_ref, invf_ref, x_ref, w_ref,
                     q1_ref, q4_ref, q16_ref, k1_ref, k4_ref, k16_ref, v1_ref, v4_ref, v16_ref, xb_ref,
                     slab_ref):
    tm = x_ref.shape[0]
    heads = N_KV_HEADS
    half = ROT_DIMS // 2
    kvw = heads * HEAD_DIM
    x = x_ref[...].astype(BF16)
    xb_ref[...] = x

    ang = pos_ref[...].astype(F32) * invf_ref[...]
    lane = lax.broadcasted_iota(jnp.int32, ang.shape, 1)
    sin = jnp.sin(ang)
    cos = jnp.cos(ang)
    sin_lo = jnp.where(lane < half, -sin, 0.0)
    sin_hi = jnp.where(lane >= half, sin, 0.0)

    def rotate(t):
        return t * cos + pltpu.roll(t, HEAD_DIM - half, 1) * sin_lo + pltpu.roll(t, half, 1) * sin_hi

    def write_units(t, h, slab, o1_ref, o4_ref, o16_ref):
        if o1_ref is not None:
            for jb in range(tm // ATTN_SPAN):
                o1_ref[jb, h] = t[jb * ATTN_SPAN:(jb + 1) * ATTN_SPAN].astype(BF16)
        if o4_ref is None and o16_ref is None:
            return
        slab_ref[slab] = t
        for o_ref, d in ((o4_ref, DILATIONS[1]), (o16_ref, DILATIONS[2])):
            if o_ref is not None:
                for r in range(d):
                    o_ref[r, h] = slab_ref[slab, pl.ds(r, tm // d, stride=d), :].astype(BF16)

    groups = (
        (0, True, (q1_ref, None, None)), (1, True, (None, q4_ref, None)), (2, True, (None, None, q16_ref)),
        (3, True, (k1_ref, k4_ref, k16_ref)), (4, False, (v1_ref, v4_ref, v16_ref)),
    )
    for step, (j, rot, outs) in enumerate(groups):
        acc = jnp.dot(x, w_ref[:, j * kvw:(j + 1) * kvw], preferred_element_type=F32)
        for h in range(heads):
            t = acc[:, h * HEAD_DIM:(h + 1) * HEAD_DIM]
            write_units(rotate(t) if rot else t, h, (step * heads + h) % slab_ref.shape[0], *outs)


def _qkv_proj(x, pos, invf, w, *, n):
    m, d = x.shape
    tm = QKV_ROWS
    heads = N_KV_HEADS
    per_tile = ATTN_TILE // tm
    assert m % ATTN_TILE == 0 and n == 5 * heads * HEAD_DIM and w.shape[1] >= n
    units = lambda rows: jax.ShapeDtypeStruct((rows // ATTN_SPAN, heads, ATTN_SPAN, HEAD_DIM), BF16)
    flat_spec = pl.BlockSpec((tm // ATTN_SPAN, heads, ATTN_SPAN, HEAD_DIM), lambda i: (i, 0, 0, 0))
    wide_spec = pl.BlockSpec((DILATIONS[2], heads, tm // DILATIONS[2], HEAD_DIM),
                             lambda i: (i // per_tile, 0, i % per_tile, 0))
    out_specs = [flat_spec, flat_spec, wide_spec] * 3 + [pl.BlockSpec((tm, d), lambda i: (i, 0))]
    n_slabs = 4
    vmem = (2 * tm * d * 4 + tm * d * 2 + 2 * d * n * 2 + 2 * 9 * tm * heads * HEAD_DIM * 2
            + n_slabs * tm * HEAD_DIM * 4 + 8 * tm * heads * HEAD_DIM * 4 + (6 << 20))
    return pl.pallas_call(
        _qkv_proj_kernel,
        grid=(m // tm,),
        in_specs=[
            pl.BlockSpec((tm, 1), lambda i: (i, 0)),
            pl.BlockSpec((1, V7X_LANES), lambda i: (0, 0)),
            pl.BlockSpec((tm, d), lambda i: (i, 0)),
            pl.BlockSpec((d, n), lambda i: (0, 0)),
        ],
        out_specs=out_specs,
        out_shape=[units(m)] * 9 + [jax.ShapeDtypeStruct((m, d), BF16)],
        scratch_shapes=[pltpu.VMEM((n_slabs, tm, HEAD_DIM), F32)],
        compiler_params=_compiler_params(("parallel",), vmem),
        name="qkv_proj",
    )(pos, invf, x, w)


def _attn_kernel(q1_ref, q4_ref, q16_ref, k1c_ref, k4c_ref, k16c_ref, k1p_ref, k4p_ref, k16p_ref,
                 v1c_ref, v4c_ref, v16c_ref, v1p_ref, v4p_ref, v16p_ref, o_ref, acc_ref, lse_ref):
    first_tile = pl.program_id(2) == 0
    blk = ATTN_SPAN
    nu = ATTN_UNITS
    scale = HEAD_DIM ** -0.5
    qi = lax.broadcasted_iota(jnp.int32, (1, blk, blk), 1)
    ki = lax.broadcasted_iota(jnp.int32, (1, blk, blk), 2)
    unit = lax.broadcasted_iota(jnp.int32, (nu, 1, 1), 0)
    neg_inf = -jnp.inf
    qk = lambda a, b: jnp.einsum("uqd,ukd->uqk", a, b, preferred_element_type=F32)
    pv = lambda a, b: jnp.einsum("uqk,ukd->uqd", a, b, preferred_element_type=F32)

    def with_prev(fn, lhs, prev_ref, cur_ref, d):
        head = fn(lhs[:d], prev_ref[...])
        if d == nu:
            return head
        return jnp.concatenate([head, fn(lhs[d:], cur_ref[0:nu - d])], axis=0)

    refs = ((q1_ref, k1c_ref, k1p_ref, v1c_ref, v1p_ref), (q4_ref, k4c_ref, k4p_ref, v4c_ref, v4p_ref),
            (q16_ref, k16c_ref, k16p_ref, v16c_ref, v16p_ref))
    for p, (d, (q_ref, kc_ref, kp_ref, vc_ref, vp_ref)) in enumerate(zip(DILATIONS, refs)):
        q = q_ref[...]
        sp = with_prev(qk, q, kp_ref, kc_ref, d) * scale
        sc = qk(q, kc_ref[...]) * scale
        sp = jnp.where(ki >= qi, sp, neg_inf)
        sp = jnp.where(jnp.logical_and(first_tile, unit < d), neg_inf, sp)
        sc = jnp.where(ki <= qi, sc, neg_inf)
        mx = jnp.max(jnp.maximum(sp, sc), axis=-1, keepdims=True)
        ep = jnp.exp(sp - mx)
        ec = jnp.exp(sc - mx)
        den = jnp.sum(ep + ec, axis=-1, keepdims=True)
        out = (with_prev(pv, ep.astype(BF16), vp_ref, vc_ref, d) + pv(ec.astype(BF16), vc_ref[...])) / den
        lse = jnp.broadcast_to(mx + jnp.log(den), out.shape)
        if d == 1:
            acc_ref[p] = out.reshape(ATTN_TILE, HEAD_DIM)
            lse_ref[p] = lse.reshape(ATTN_TILE, HEAD_DIM)
        else:
            for u in range(nu):
                jb, r = divmod(u, d) if d < nu else (0, u)
                rows = pl.ds(d * blk * jb + r, blk, stride=d)
                acc_ref[p, rows, :] = out[u]
                lse_ref[p, rows, :] = lse[u]

    lse = lse_ref[...]
    w = jnp.exp(lse - jnp.max(lse, axis=0, keepdims=True))
    merged = jnp.sum(w * acc_ref[...], axis=0) / jnp.sum(w, axis=0)
    o_ref[...] = merged.astype(o_ref.dtype)


def _attention(qkv, *, batch, seq):
    nt = seq // ATTN_TILE
    heads = N_KV_HEADS
    nu = ATTN_UNITS
    q1, q4, q16, k1, k4, k16, v1, v4, v16 = qkv

    def cur():
        return pl.BlockSpec((nu, None, ATTN_SPAN, HEAD_DIM), lambda b, h, n: (b * nt + n, h, 0, 0))

    def prev(d):
        per = nu // d
        return pl.BlockSpec((d, None, ATTN_SPAN, HEAD_DIM),
                            lambda b, h, n: (jnp.maximum((b * nt + n) * per - 1, 0), h, 0, 0))

    unit_bytes = ATTN_SPAN * HEAD_DIM * 2
    tile_f32 = ATTN_TILE * HEAD_DIM * 4
    vmem = 2 * (9 * nu + 2 * sum(DILATIONS)) * unit_bytes + 2 * tile_f32 // 2 + 6 * tile_f32 + 12 * tile_f32 + (4 << 20)
    return pl.pallas_call(
        _attn_kernel,
        grid=(batch, heads, nt),
        in_specs=[cur(), cur(), cur(), cur(), cur(), cur(), prev(1), prev(4), prev(16),
                  cur(), cur(), cur(), prev(1), prev(4), prev(16)],
        out_specs=pl.BlockSpec((ATTN_TILE, HEAD_DIM), lambda b, h, n: (b * nt + n, h)),
        out_shape=jax.ShapeDtypeStruct((batch * seq, heads * HEAD_DIM), BF16),
        scratch_shapes=[pltpu.VMEM((len(DILATIONS), ATTN_TILE, HEAD_DIM), F32)] * 2,
        compiler_params=_compiler_params(("parallel", "parallel", "arbitrary"), vmem),
        name="dilated_attention",
    )(q1, q4, q16, k1, k4, k16, k1, k4, k16, v1, v4, v16, v1, v4, v16)


SCAN_PAD_ROWS = V7X_SUBLANES


def _linear_scan_rows(a, u, h0, sa_ref, su_ref):
    rows, width = a.shape
    sub = V7X_SUBLANES
    run = rows // sub
    pitch = run + SCAN_PAD_ROWS
    lanes = V7X_LANES
    out = []
    carry = []
    for c in range(width // lanes):
        cols = slice(c * lanes, (c + 1) * lanes)
        for s in range(sub):
            sa_ref[c, s * pitch:s * pitch + run, :] = a[s * run:(s + 1) * run, cols]
            su_ref[c, s * pitch:s * pitch + run, :] = u[s * run:(s + 1) * run, cols]
        h = jnp.zeros((sub, lanes), F32)
        prod = jnp.ones((sub, lanes), F32)
        hs, prods = [], []
        for k in range(run):
            ak = sa_ref[c, pl.ds(k, sub, stride=pitch), :]
            uk = su_ref[c, pl.ds(k, sub, stride=pitch), :]
            h = ak * h + uk
            prod = ak * prod
            hs.append(h)
            prods.append(prod)
        start = h0[:, cols]
        enter = [start]
        for s in range(sub - 1):
            start = h[s:s + 1, :] + prod[s:s + 1, :] * start
            enter.append(start)
        carry.append(h[sub - 1:sub, :] + prod[sub - 1:sub, :] * start)
        enter = jnp.concatenate(enter, axis=0)
        for k in range(run):
            su_ref[c, pl.ds(k, sub, stride=pitch), :] = hs[k] + prods[k] * enter
        out.append(jnp.concatenate([su_ref[c, s * pitch:s * pitch + run, :] for s in range(sub)], axis=0))
    return jnp.concatenate(out, axis=1), jnp.concatenate(carry, axis=1)


def _gelu_tanh(y):
    z = 0.7978845608028654 * (y + 0.044715 * (y * y * y))
    return y * jax.nn.sigmoid(2.0 * z)


def _lru_kernel(hin_ref, wx_ref, wy_ref, cw_ref, cb_ref, wri_ref, br_ref, bi_ref, lam_ref, o_ref,
                wxb_ref, wyb_ref, xs_ref, ys_ref, h_ref, sa_ref, su_ref):
    tl = hin_ref.shape[0]
    pad = V7X_SUBLANES
    s = pl.program_id(2)

    @pl.when(s == 0)
    def _():
        wxb_ref[...] = wx_ref[...].astype(BF16)
        wyb_ref[...] = wy_ref[...].astype(BF16)
        xs_ref[...] = jnp.zeros_like(xs_ref)
        ys_ref[...] = jnp.zeros_like(ys_ref)

    @pl.when(s <= 1)
    def _():
        xs_ref[0:pad, :] = jnp.zeros((pad, xs_ref.shape[1]), F32)
        h_ref[...] = jnp.zeros_like(h_ref)

    half = wxb_ref.shape[1] // 2

    def project(dst_ref, rows, w_ref, part):
        cols = slice(part * half, (part + 1) * half)
        dst_ref[rows, cols] = jnp.dot(hin_ref[...], w_ref[:, cols], preferred_element_type=F32)

    x = xs_ref[pad:pad + tl, :]
    y = ys_ref[...]
    project(ys_ref, slice(0, tl), wyb_ref, 0)
    xc = cw_ref[CONV_WIDTH - 1:CONV_WIDTH, :] * x
    for j in range(CONV_WIDTH - 1):
        shift = CONV_WIDTH - 1 - j
        xc = xc + cw_ref[j:j + 1, :] * xs_ref[pad - shift:pad - shift + tl, :]
    xc = xc + cb_ref[...]
    xs_ref[0:pad, :] = x[tl - pad:tl]

    bw = LRU_BLOCK_WIDTH
    pre = [jnp.dot(xc[:, g * bw:(g + 1) * bw].astype(BF16), wri_ref[g], preferred_element_type=F32)
           for g in range(x.shape[1] // bw)]
    project(ys_ref, slice(0, tl), wyb_ref, 1)
    project(xs_ref, slice(pad, pad + tl), wxb_ref, 0)

    r = jax.nn.sigmoid(jnp.concatenate([t[:, :bw] for t in pre], axis=1) + br_ref[...])
    i = jax.nn.sigmoid(jnp.concatenate([t[:, bw:] for t in pre], axis=1) + bi_ref[...])
    neg_lam = -lam_ref[...]
    softplus = jnp.maximum(neg_lam, 0.0) + jnp.log1p(jnp.exp(-jnp.abs(neg_lam)))
    log_a = (-LRU_C * softplus) * r
    a = jnp.exp(log_a)
    one_m_a2 = 1.0 - a * a
    root = jnp.where(one_m_a2 > 0.0, one_m_a2 * lax.rsqrt(one_m_a2), 0.0)
    u = root * (i * xc)
    project(xs_ref, slice(pad, pad + tl), wxb_ref, 1)
    h, h_ref[...] = _linear_scan_rows(a, u, h_ref[...], sa_ref, su_ref)
    o_ref[...] = (h * _gelu_tanh(y)).astype(o_ref.dtype)


def _rg_lru(hin, w_in, conv_w, conv_b, w_ri, b_r, b_i, lam, *, batch, seq, x_col, width, tl=256, tc=512):
    m, d = hin.shape
    nt = seq // tl
    nc = width // tc
    assert seq % tl == 0 and width % tc == 0 and x_col % tc == 0
    gpc = tc // LRU_BLOCK_WIDTH
    row = pl.BlockSpec((1, tc), lambda b, c, n: (0, c))
    vmem = (2 * tl * d * 2 + 2 * 2 * d * tc * 4 + 2 * d * tc * 2 + 2 * tl * tc * 2
            + 2 * gpc * 128 * 256 * 2 + 16 * tl * tc * 4 + (4 << 20))
    return pl.pallas_call(
        _lru_kernel,
        grid=(batch, nc, nt + 1),
        in_specs=[
            pl.BlockSpec((tl, d), lambda b, c, n: (b * nt + jnp.minimum(n, nt - 1), 0)),
            pl.BlockSpec((d, tc), lambda b, c, n: (0, x_col // tc + c)),
            pl.BlockSpec((d, tc), lambda b, c, n: (0, x_col // tc + nc + c)),
            pl.BlockSpec((CONV_WIDTH, tc), lambda b, c, n: (0, c)),
            row,
            pl.BlockSpec((gpc, LRU_BLOCK_WIDTH, 2 * LRU_BLOCK_WIDTH), lambda b, c, n: (c, 0, 0)),
            row, row, row,
        ],
        out_specs=pl.BlockSpec((tl, tc), lambda b, c, n: (b * nt + jnp.maximum(n - 1, 0), c)),
        out_shape=jax.ShapeDtypeStruct((m, width), BF16),
        scratch_shapes=[pltpu.VMEM((d, tc), BF16), pltpu.VMEM((d, tc), BF16),
                        pltpu.VMEM((tl + V7X_SUBLANES, tc), F32), pltpu.VMEM((tl, tc), F32),
                        pltpu.VMEM((1, tc), F32)]
        + [pltpu.VMEM((tc // V7X_LANES, tl + V7X_SUBLANES * SCAN_PAD_ROWS, V7X_LANES), F32)] * 2,
        compiler_params=_compiler_params(("parallel", "parallel", "arbitrary"), vmem),
        name="rg_lru",
    )(hin, w_in, w_in, conv_w, conv_b, w_ri, b_r, b_i, lam)


def _resident(shape):
    return pl.BlockSpec(shape, lambda i: (0,) * len(shape), pipeline_mode=pl.Buffered(1))


def _out_ln_kernel(x_ref, a_ref, r_ref, w_ref, g_ref, b_ref, o_ref, wb_ref, *, alpha, row_chunks):
    @pl.when(pl.program_id(0) == 0)
    def _():
        wb_ref[...] = w_ref[...].astype(BF16)

    rows = x_ref.shape[0] // row_chunks
    ka = a_ref.shape[1]
    for c in range(row_chunks):
        sl = slice(c * rows, (c + 1) * rows)
        mix = (jnp.dot(a_ref[sl, :], wb_ref[:ka, :], preferred_element_type=F32)
               + jnp.dot(r_ref[sl, :], wb_ref[ka:, :], preferred_element_type=F32))
        o_ref[sl, :] = _layer_norm_rows(alpha * x_ref[sl, :] + mix, g_ref[...], b_ref[...])


def _out_ln(x, attn, rec, w, g, b, *, alpha, tm=512, row_chunks=2):
    m, d = x.shape
    ka, kr = attn.shape[1], rec.shape[1]
    assert m % tm == 0 and tm % row_chunks == 0 and w.shape == (ka + kr, d)
    vmem = (2 * 2 * tm * d * 4 + 2 * tm * (ka + kr) * 2 + (ka + kr) * d * (4 + 2) + 3 * tm * d * 4 + (4 << 20))
    return pl.pallas_call(
        functools.partial(_out_ln_kernel, alpha=alpha, row_chunks=row_chunks),
        grid=(m // tm,),
        in_specs=[
            pl.BlockSpec((tm, d), lambda i: (i, 0)),
            pl.BlockSpec((tm, ka), lambda i: (i, 0)),
            pl.BlockSpec((tm, kr), lambda i: (i, 0)),
            _resident((ka + kr, d)), _resident((1, d)), _resident((1, d)),
        ],
        out_specs=pl.BlockSpec((tm, d), lambda i: (i, 0)),
        out_shape=jax.ShapeDtypeStruct((m, d), F32),
        scratch_shapes=[pltpu.VMEM((ka + kr, d), BF16)],
        compiler_params=_compiler_params(("arbitrary",), vmem),
        name="out_ln",
    )(x, attn, rec, w, g, b)


def _ple_kernel(x_ref, p_ref, wg_ref, wp_ref, o_ref, wgb_ref, wpb_ref):
    @pl.when(pl.program_id(0) == 0)
    def _():
        wgb_ref[...] = wg_ref[...].astype(BF16)
        wpb_ref[...] = wp_ref[...].astype(BF16)

    x = x_ref[...]
    gate = jax.nn.sigmoid(jnp.dot(x.astype(BF16), wgb_ref[...], preferred_element_type=F32))
    emb = jnp.dot(p_ref[...].astype(BF16), wpb_ref[...], preferred_element_type=F32)
    o_ref[...] = x + gate * emb


def _ple(x, p, wg, wp, *, tm=512):
    m, d = x.shape
    dp = p.shape[1]
    assert m % tm == 0
    vmem = 2 * 2 * tm * d * 4 + 2 * tm * dp * 4 + (d + dp) * d * (4 + 2) + 3 * tm * d * 4 + (4 << 20)
    return pl.pallas_call(
        _ple_kernel,
        grid=(m // tm,),
        in_specs=[
            pl.BlockSpec((tm, d), lambda i: (i, 0)),
            pl.BlockSpec((tm, dp), lambda i: (i, 0)),
            _resident((d, d)), _resident((dp, d)),
        ],
        out_specs=pl.BlockSpec((tm, d), lambda i: (i, 0)),
        out_shape=jax.ShapeDtypeStruct((m, d), F32),
        scratch_shapes=[pltpu.VMEM((d, d), BF16), pltpu.VMEM((dp, d), BF16)],
        compiler_params=_compiler_params(("arbitrary",), vmem),
        name="ple",
    )(x, p, wg, wp)


def kernel(x, p, positions, ffn1_w_gate, ffn1_w_up, ffn1_w_down, ln1_g, ln1_b, w_in, conv_w, conv_b, w_rgate, b_rgate, w_igate, b_igate, lru_lambda, w_out, ln2_g, ln2_b, ffn2_w_gate, ffn2_w_up, ffn2_w_down, ln3_g, ln3_b, w_ple_proj, w_ple_gate):
    batch, seq, d_model = x.shape
    depth = ffn1_w_gate.shape[0]
    m = batch * seq
    kv_width = N_KV_HEADS * HEAD_DIM
    qkv_width = (len(DILATIONS) + 2) * kv_width
    lru_width = d_model - kv_width
    alpha = (2.0 * depth) ** 0.25
    assert w_in.shape[2] == qkv_width + 2 * lru_width

    half = ROT_DIMS // 2
    inv_freq = jnp.power(jnp.float32(ROPE_THETA), -jnp.arange(half, dtype=F32) * (2.0 / ROT_DIMS))
    invf = jnp.zeros((1, V7X_LANES), F32).at[0, :ROT_DIMS].set(jnp.tile(inv_freq, 2))
    pos = positions.reshape(m, 1)
    row = lambda v: v.reshape(1, -1)

    h = x.reshape(m, d_model)
    for i in range(depth):
        h, (*ffn2_w, w_in_bf16) = _ffn_ln(
            h, ffn1_w_gate[i].astype(BF16), ffn1_w_up[i].astype(BF16), ffn1_w_down[i].astype(BF16),
            row(ln1_g[i]), row(ln1_b[i]), alpha=alpha,
            cast_payload=(ffn2_w_gate[i], ffn2_w_up[i], ffn2_w_down[i], w_in[i]))
        *qkv, hb = _qkv_proj(h, pos, invf, w_in_bf16, n=qkv_width)
        attn = _attention(qkv, batch=batch, seq=seq)
        w_ri = jnp.concatenate([w_rgate[i], w_igate[i]], axis=-1).astype(BF16)
        rec = _rg_lru(hb, w_in[i], conv_w[i], row(conv_b[i]), w_ri, row(b_rgate[i]), row(b_igate[i]),
                      row(lru_lambda[i]), batch=batch, seq=seq, x_col=qkv_width, width=lru_width)
        h = _out_ln(h, attn, rec, w_out[i], row(ln2_g[i]), row(ln2_b[i]), alpha=alpha)
        h, _ = _ffn_ln(h, *ffn2_w, row(ln3_g[i]), row(ln3_b[i]), alpha=alpha)
        h = _ple(h, p[i].reshape(m, -1), w_ple_gate[i], w_ple_proj[i])
    return h.reshape(batch, seq, d_model)
```

```python
import functools

import jax
import jax.numpy as jnp
from jax import lax
from jax.experimental import pallas as pl
from jax.experimental.pallas import tpu as pltpu

F32 = jnp.float32
BF16 = jnp.bfloat16

HEAD_DIM = 128
N_KV_HEADS = 4
DILATIONS = (1, 4, 16)
ATTN_SPAN = 128
ROT_DIMS = HEAD_DIM // 4
ROPE_THETA = 500000.0
LRU_BLOCK_WIDTH = 128
CONV_WIDTH = 4
LRU_C = 8.0
LN_EPS = 1e-5

V7X_LANES = 128
V7X_SUBLANES = 8
V7X_VMEM_SCOPED_BYTES = 60000 * 1024

ATTN_TILE = ATTN_SPAN * DILATIONS[-1]
ATTN_UNITS = ATTN_TILE // ATTN_SPAN
QKV_ROWS = ATTN_SPAN * DILATIONS[1]


def _compiler_params(semantics, vmem_estimate_bytes):
    limit = min(int(vmem_estimate_bytes), V7X_VMEM_SCOPED_BYTES)
    return pltpu.CompilerParams(dimension_semantics=semantics, vmem_limit_bytes=limit)


def _layer_norm_rows(y, g, b):
    mu = jnp.mean(y, axis=-1, keepdims=True)
    yc = y - mu
    var = jnp.mean(yc * yc, axis=-1, keepdims=True)
    return yc * lax.rsqrt(var + LN_EPS) * g + b


def _ffn_ln_kernel(x_ref, wg_ref, wu_ref, wd_ref, g_ref, b_ref, *rest, alpha, ln_row_chunks, n_cast):
    cast_in, o_ref, cast_out = rest[:n_cast], rest[n_cast], rest[n_cast + 1:]
    f = pl.program_id(1)
    last = pl.num_programs(1) - 1

    def cast_payload():
        for src, dst in zip(cast_in, cast_out):
            dst[...] = src[...].astype(BF16)

    def partial_sum(rows):
        xb = x_ref[rows, :].astype(BF16)
        gate = jnp.dot(xb, wg_ref[...], preferred_element_type=F32)
        up = jnp.dot(xb, wu_ref[...], preferred_element_type=F32)
        act = (jax.nn.silu(gate) * up).astype(BF16)
        return jnp.dot(act, wd_ref[...], preferred_element_type=F32)

    def accumulated(rows):
        return o_ref[rows, :] + partial_sum(rows)

    @pl.when(f == 0)
    def _():
        cast_payload()
        o_ref[...] = partial_sum(slice(None))

    @pl.when(jnp.logical_and(f > 0, f < last))
    def _():
        cast_payload()
        o_ref[...] = accumulated(slice(None))

    @pl.when(f == last)
    def _():
        cast_payload()
        rows_per = x_ref.shape[0] // ln_row_chunks
        for c in range(ln_row_chunks):
            rows = slice(c * rows_per, (c + 1) * rows_per)
            y = alpha * x_ref[rows, :] + 0.5 * accumulated(rows)
            o_ref[rows, :] = _layer_norm_rows(y, g_ref[...], b_ref[...])


def _ffn_ln(x, wg, wu, wd, g, b, *, alpha, cast_payload=(), layer=0, tm=1024, tf=512, ln_row_chunks=4):
    m, d = x.shape
    dff = wg.shape[1]
    assert m % tm == 0 and dff % tf == 0
    n_tiles, nf = m // tm, dff // tf
    pay_in, pay_out = [], []
    for a in cast_payload:
        _, rows, cols = a.shape
        if rows % n_tiles == 0 and cols % nf == 0:
            blk, idx = (rows // n_tiles, cols // nf), (lambda i, f: (i, f))
        else:
            blk, idx = (rows // nf, cols // n_tiles), (lambda i, f: (f, i))
        assert blk[0] * blk[1] * n_tiles * nf == rows * cols
        assert blk[0] % (2 * V7X_SUBLANES) == 0 and blk[1] % V7X_LANES == 0, blk
        pay_in.append(pl.BlockSpec((None,) + blk, lambda i, f, idx=idx: (layer,) + idx(i, f)))
        pay_out.append(pl.BlockSpec(blk, idx))
    vmem = (2 * tm * d * (4 + 4) + tm * d * 2 + 2 * 3 * d * tf * 2
            + 3 * tm * tf * 4 + (6 << 20)
            + sum(2 * s.block_shape[0] * s.block_shape[1] * (4 + 2) for s in pay_out))
    out, *cast = pl.pallas_call(
        functools.partial(_ffn_ln_kernel, alpha=alpha, ln_row_chunks=ln_row_chunks, n_cast=len(cast_payload)),
        grid=(n_tiles, nf),
        in_specs=[
            pl.BlockSpec((tm, d), lambda i, f: (i, 0)),
            pl.BlockSpec((d, tf), lambda i, f: (0, f)),
            pl.BlockSpec((d, tf), lambda i, f: (0, f)),
            pl.BlockSpec((tf, d), lambda i, f: (f, 0)),
            pl.BlockSpec((1, d), lambda i, f: (0, 0)),
            pl.BlockSpec((1, d), lambda i, f: (0, 0)),
        ] + pay_in,
        out_specs=[pl.BlockSpec((tm, d), lambda i, f: (i, 0))] + pay_out,
        out_shape=[jax.ShapeDtypeStruct((m, d), F32)]
        + [jax.ShapeDtypeStruct(a.shape[1:], BF16) for a in cast_payload],
        compiler_params=_compiler_params(("parallel", "arbitrary"), vmem),
        name="ffn_ln",
    )(x, wg, wu, wd, g, b, *cast_payload)
    return out, cast


def _qkv_proj_kernel(pos_ref, invf_ref, x_ref, w_ref,
                     q1_ref, q4_ref, q16_ref, k1_ref, k4_ref, k16_ref, v1_ref, v4_ref, v16_ref, xb_ref,
                     slab_ref):
    tm = x_ref.shape[0]
    heads = N_KV_HEADS
    half = ROT_DIMS // 2
    kvw = heads * HEAD_DIM
    x = x_ref[...].astype(BF16)
    xb_ref[...] = x

    ang = pos_ref[...].astype(F32) * invf_ref[...]
    lane = lax.broadcasted_iota(jnp.int32, ang.shape, 1)
    sin = jnp.sin(ang)
    cos = jnp.cos(ang)
    sin_lo = jnp.where(lane < half, -sin, 0.0)
    sin_hi = jnp.where(lane >= half, sin, 0.0)

    def rotate(t):
        return t * cos + pltpu.roll(t, HEAD_DIM - half, 1) * sin_lo + pltpu.roll(t, half, 1) * sin_hi

    def write_units(t, h, slab, o1_ref, o4_ref, o16_ref):
        if o1_ref is not None:
            for jb in range(tm // ATTN_SPAN):
                o1_ref[jb, h] = t[jb * ATTN_SPAN:(jb + 1) * ATTN_SPAN].astype(BF16)
        if o4_ref is None and o16_ref is None:
            return
        slab_ref[slab] = t
        for o_ref, d in ((o4_ref, DILATIONS[1]), (o16_ref, DILATIONS[2])):
            if o_ref is not None:
                for r in range(d):
                    o_ref[r, h] = slab_ref[slab, pl.ds(r, tm // d, stride=d), :].astype(BF16)

    groups = (
        (0, True, (q1_ref, None, None)), (1, True, (None, q4_ref, None)), (2, True, (None, None, q16_ref)),
        (3, True, (k1_ref, k4_ref, k16_ref)), (4, False, (v1_ref, v4_ref, v16_ref)),
    )
    for step, (j, rot, outs) in enumerate(groups):
        acc = jnp.dot(x, w_ref[:, j * kvw:(j + 1) * kvw], preferred_element_type=F32)
        for h in range(heads):
            t = acc[:, h * HEAD_DIM:(h + 1) * HEAD_DIM]
            write_units(rotate(t) if rot else t, h, (step * heads + h) % slab_ref.shape[0], *outs)


def _qkv_proj(x, pos, invf, w, *, n):
    m, d = x.shape
    tm = QKV_ROWS
    heads = N_KV_HEADS
    per_tile = ATTN_TILE // tm
    assert m % ATTN_TILE == 0 and n == 5 * heads * HEAD_DIM and w.shape[1] >= n
    units = lambda rows: jax.ShapeDtypeStruct((rows // ATTN_SPAN, heads, ATTN_SPAN, HEAD_DIM), BF16)
    flat_spec = pl.BlockSpec((tm // ATTN_SPAN, heads, ATTN_SPAN, HEAD_DIM), lambda i: (i, 0, 0, 0))
    wide_spec = pl.BlockSpec((DILATIONS[2], heads, tm // DILATIONS[2], HEAD_DIM),
                             lambda i: (i // per_tile, 0, i % per_tile, 0))
    out_specs = [flat_spec, flat_spec, wide_spec] * 3 + [pl.BlockSpec((tm, d), lambda i: (i, 0))]
    n_slabs = 4
    vmem = (2 * tm * d * 4 + tm * d * 2 + 2 * d * n * 2 + 2 * 9 * tm * heads * HEAD_DIM * 2
            + n_slabs * tm * HEAD_DIM * 4 + 8 * tm * heads * HEAD_DIM * 4 + (6 << 20))
    return pl.pallas_call(
        _qkv_proj_kernel,
        grid=(m // tm,),
        in_specs=[
            pl.BlockSpec((tm, 1), lambda i: (i, 0)),
            pl.BlockSpec((1, V7X_LANES), lambda i: (0, 0)),
            pl.BlockSpec((tm, d), lambda i: (i, 0)),
            pl.BlockSpec((d, n), lambda i: (0, 0)),
        ],
        out_specs=out_specs,
        out_shape=[units(m)] * 9 + [jax.ShapeDtypeStruct((m, d), BF16)],
        scratch_shapes=[pltpu.VMEM((n_slabs, tm, HEAD_DIM), F32)],
        compiler_params=_compiler_params(("parallel",), vmem),
        name="qkv_proj",
    )(pos, invf, x, w)


def _attn_kernel(q1_ref, q4_ref, q16_ref, k1c_ref, k4c_ref, k16c_ref, k1p_ref, k4p_ref, k16p_ref,
                 v1c_ref, v4c_ref, v16c_ref, v1p_ref, v4p_ref, v16p_ref, o_ref, acc_ref, lse_ref):
    first_tile = pl.program_id(2) == 0
    blk = ATTN_SPAN
    nu = ATTN_UNITS
    scale = HEAD_DIM ** -0.5
    qi = lax.broadcasted_iota(jnp.int32, (1, blk, blk), 1)
    ki = lax.broadcasted_iota(jnp.int32, (1, blk, blk), 2)
    unit = lax.broadcasted_iota(jnp.int32, (nu, 1, 1), 0)
    neg_inf = -jnp.inf
    qk = lambda a, b: jnp.einsum("uqd,ukd->uqk", a, b, preferred_element_type=F32)
    pv = lambda a, b: jnp.einsum("uqk,ukd->uqd", a, b, preferred_element_type=F32)

    def with_prev(fn, lhs, prev_ref, cur_ref, d):
        head = fn(lhs[:d], prev_ref[...])
        if d == nu:
            return head
        return jnp.concatenate([head, fn(lhs[d:], cur_ref[0:nu - d])], axis=0)

    refs = ((q1_ref, k1c_ref, k1p_ref, v1c_ref, v1p_ref), (q4_ref, k4c_ref, k4p_ref, v4c_ref, v4p_ref),
            (q16_ref, k16c_ref, k16p_ref, v16c_ref, v16p_ref))
    for p, (d, (q_ref, kc_ref, kp_ref, vc_ref, vp_ref)) in enumerate(zip(DILATIONS, refs)):
        q = q_ref[...]
        sp = with_prev(qk, q, kp_ref, kc_ref, d) * scale
        sc = qk(q, kc_ref[...]) * scale
        sp = jnp.where(ki >= qi, sp, neg_inf)
        sp = jnp.where(jnp.logical_and(first_tile, unit < d), neg_inf, sp)
        sc = jnp.where(ki <= qi, sc, neg_inf)
        mx = jnp.max(jnp.maximum(sp, sc), axis=-1, keepdims=True)
        ep = jnp.exp(sp - mx)
        ec = jnp.exp(sc - mx)
        den = jnp.sum(ep + ec, axis=-1, keepdims=True)
        out = (with_prev(pv, ep.astype(BF16), vp_ref, vc_ref, d) + pv(ec.astype(BF16), vc_ref[...])) / den
        lse = jnp.broadcast_to(mx + jnp.log(den), out.shape)
        if d == 1:
            acc_ref[p] = out.reshape(ATTN_TILE, HEAD_DIM)
            lse_ref[p] = lse.reshape(ATTN_TILE, HEAD_DIM)
        else:
            for u in range(nu):
                jb, r = divmod(u, d) if d < nu else (0, u)
                rows = pl.ds(d * blk * jb + r, blk, stride=d)
                acc_ref[p, rows, :] = out[u]
                lse_ref[p, rows, :] = lse[u]

    lse = lse_ref[...]
    w = jnp.exp(lse - jnp.max(lse, axis=0, keepdims=True))
    merged = jnp.sum(w * acc_ref[...], axis=0) / jnp.sum(w, axis=0)
    o_ref[...] = merged.astype(o_ref.dtype)


def _attention(qkv, *, batch, seq):
    nt = seq // ATTN_TILE
    heads = N_KV_HEADS
    nu = ATTN_UNITS
    q1, q4, q16, k1, k4, k16, v1, v4, v16 = qkv

    def cur():
        return pl.BlockSpec((nu, None, ATTN_SPAN, HEAD_DIM), lambda b, h, n: (b * nt + n, h, 0, 0))

    def prev(d):
        per = nu // d
        return pl.BlockSpec((d, None, ATTN_SPAN, HEAD_DIM),
                            lambda b, h, n: (jnp.maximum((b * nt + n) * per - 1, 0), h, 0, 0))

    unit_bytes = ATTN_SPAN * HEAD_DIM * 2
    tile_f32 = ATTN_TILE * HEAD_DIM * 4
    vmem = 2 * (9 * nu + 2 * sum(DILATIONS)) * unit_bytes + 2 * tile_f32 // 2 + 6 * tile_f32 + 12 * tile_f32 + (4 << 20)
    return pl.pallas_call(
        _attn_kernel,
        grid=(batch, heads, nt),
        in_specs=[cur(), cur(), cur(), cur(), cur(), cur(), prev(1), prev(4), prev(16),
                  cur(), cur(), cur(), prev(1), prev(4), prev(16)],
        out_specs=pl.BlockSpec((ATTN_TILE, HEAD_DIM), lambda b, h, n: (b * nt + n, h)),
        out_shape=jax.ShapeDtypeStruct((batch * seq, heads * HEAD_DIM), BF16),
        scratch_shapes=[pltpu.VMEM((len(DILATIONS), ATTN_TILE, HEAD_DIM), F32)] * 2,
        compiler_params=_compiler_params(("parallel", "parallel", "arbitrary"), vmem),
        name="dilated_attention",
    )(q1, q4, q16, k1, k4, k16, k1, k4, k16, v1, v4, v16, v1, v4, v16)


SCAN_PAD_ROWS = V7X_SUBLANES


def _linear_scan_rows(a, u, h0, sa_ref, su_ref):
    rows, width = a.shape
    sub = V7X_SUBLANES
    run = rows // sub
    pitch = run + SCAN_PAD_ROWS
    lanes = V7X_LANES
    out = []
    carry = []
    for c in range(width // lanes):
        cols = slice(c * lanes, (c + 1) * lanes)
        for s in range(sub):
            sa_ref[c, s * pitch:s * pitch + run, :] = a[s * run:(s + 1) * run, cols]
            su_ref[c, s * pitch:s * pitch + run, :] = u[s * run:(s + 1) * run, cols]
        h = jnp.zeros((sub, lanes), F32)
        prod = jnp.ones((sub, lanes), F32)
        hs, prods = [], []
        for k in range(run):
            ak = sa_ref[c, pl.ds(k, sub, stride=pitch), :]
            uk = su_ref[c, pl.ds(k, sub, stride=pitch), :]
            h = ak * h + uk
            prod = ak * prod
            hs.append(h)
            prods.append(prod)
        start = h0[:, cols]
        enter = [start]
        for s in range(sub - 1):
            start = h[s:s + 1, :] + prod[s:s + 1, :] * start
            enter.append(start)
        carry.append(h[sub - 1:sub, :] + prod[sub - 1:sub, :] * start)
        enter = jnp.concatenate(enter, axis=0)
        for k in range(run):
            su_ref[c, pl.ds(k, sub, stride=pitch), :] = hs[k] + prods[k] * enter
        out.append(jnp.concatenate([su_ref[c, s * pitch:s * pitch + run, :] for s in range(sub)], axis=0))
    return jnp.concatenate(out, axis=1), jnp.concatenate(carry, axis=1)


def _gelu_tanh(y):
    z = 0.7978845608028654 * (y + 0.044715 * (y * y * y))
    return y * jax.nn.sigmoid(2.0 * z)


def _lru_kernel(hin_ref, wx_ref, wy_ref, cw_ref, cb_ref, wri_ref, br_ref, bi_ref, lam_ref, o_ref,
                wxb_ref, wyb_ref, xs_ref, ys_ref, h_ref, sa_ref, su_ref):
    tl = hin_ref.shape[0]
    pad = V7X_SUBLANES
    s = pl.program_id(2)

    @pl.when(s == 0)
    def _():
        wxb_ref[...] = wx_ref[...].astype(BF16)
        wyb_ref[...] = wy_ref[...].astype(BF16)
        xs_ref[...] = jnp.zeros_like(xs_ref)
        ys_ref[...] = jnp.zeros_like(ys_ref)

    @pl.when(s <= 1)
    def _():
        xs_ref[0:pad, :] = jnp.zeros((pad, xs_ref.shape[1]), F32)
        h_ref[...] = jnp.zeros_like(h_ref)

    half = wxb_ref.shape[1] // 2

    def project(dst_ref, rows, w_ref, part):
        cols = slice(part * half, (part + 1) * half)
        dst_ref[rows, cols] = jnp.dot(hin_ref[...], w_ref[:, cols], preferred_element_type=F32)

    x = xs_ref[pad:pad + tl, :]
    y = ys_ref[...]
    project(ys_ref, slice(0, tl), wyb_ref, 0)
    xc = cw_ref[CONV_WIDTH - 1:CONV_WIDTH, :] * x
    for j in range(CONV_WIDTH - 1):
        shift = CONV_WIDTH - 1 - j
        xc = xc + cw_ref[j:j + 1, :] * xs_ref[pad - shift:pad - shift + tl, :]
    xc = xc + cb_ref[...]
    xs_ref[0:pad, :] = x[tl - pad:tl]

    bw = LRU_BLOCK_WIDTH
    pre = [jnp.dot(xc[:, g * bw:(g + 1) * bw].astype(BF16), wri_ref[g], preferred_element_type=F32)
           for g in range(x.shape[1] // bw)]
    project(ys_ref, slice(0, tl), wyb_ref, 1)
    project(xs_ref, slice(pad, pad + tl), wxb_ref, 0)

    r = jax.nn.sigmoid(jnp.concatenate([t[:, :bw] for t in pre], axis=1) + br_ref[...])
    i = jax.nn.sigmoid(jnp.concatenate([t[:, bw:] for t in pre], axis=1) + bi_ref[...])
    neg_lam = -lam_ref[...]
    softplus = jnp.maximum(neg_lam, 0.0) + jnp.log1p(jnp.exp(-jnp.abs(neg_lam)))
    log_a = (-LRU_C * softplus) * r
    a = jnp.exp(log_a)
    one_m_a2 = 1.0 - a * a
    root = jnp.where(one_m_a2 > 0.0, one_m_a2 * lax.rsqrt(one_m_a2), 0.0)
    u = root * (i * xc)
    project(xs_ref, slice(pad, pad + tl), wxb_ref, 1)
    h, h_ref[...] = _linear_scan_rows(a, u, h_ref[...], sa_ref, su_ref)
    o_ref[...] = (h * _gelu_tanh(y)).astype(o_ref.dtype)


def _rg_lru(hin, w_in, conv_w, conv_b, w_ri, b_r, b_i, lam, *, batch, seq, x_col, width, tl=256, tc=512):
    m, d = hin.shape
    nt = seq // tl
    nc = width // tc
    assert seq % tl == 0 and width % tc == 0 and x_col % tc == 0
    gpc = tc // LRU_BLOCK_WIDTH
    row = pl.BlockSpec((1, tc), lambda b, c, n: (0, c))
    vmem = (2 * tl * d * 2 + 2 * 2 * d * tc * 4 + 2 * d * tc * 2 + 2 * tl * tc * 2
            + 2 * gpc * 128 * 256 * 2 + 16 * tl * tc * 4 + (4 << 20))
    return pl.pallas_call(
        _lru_kernel,
        grid=(batch, nc, nt + 1),
        in_specs=[
            pl.BlockSpec((tl, d), lambda b, c, n: (b * nt + jnp.minimum(n, nt - 1), 0)),
            pl.BlockSpec((d, tc), lambda b, c, n: (0, x_col // tc + c)),
            pl.BlockSpec((d, tc), lambda b, c, n: (0, x_col // tc + nc + c)),
            pl.BlockSpec((CONV_WIDTH, tc), lambda b, c, n: (0, c)),
            row,
            pl.BlockSpec((gpc, LRU_BLOCK_WIDTH, 2 * LRU_BLOCK_WIDTH), lambda b, c, n: (c, 0, 0)),
            row, row, row,
        ],
        out_specs=pl.BlockSpec((tl, tc), lambda b, c, n: (b * nt + jnp.maximum(n - 1, 0), c)),
        out_shape=jax.ShapeDtypeStruct((m, width), BF16),
        scratch_shapes=[pltpu.VMEM((d, tc), BF16), pltpu.VMEM((d, tc), BF16),
                        pltpu.VMEM((tl + V7X_SUBLANES, tc), F32), pltpu.VMEM((tl, tc), F32),
                        pltpu.VMEM((1, tc), F32)]
        + [pltpu.VMEM((tc // V7X_LANES, tl + V7X_SUBLANES * SCAN_PAD_ROWS, V7X_LANES), F32)] * 2,
        compiler_params=_compiler_params(("parallel", "parallel", "arbitrary"), vmem),
        name="rg_lru",
    )(hin, w_in, w_in, conv_w, conv_b, w_ri, b_r, b_i, lam)


def _resident(shape):
    return pl.BlockSpec(shape, lambda i: (0,) * len(shape), pipeline_mode=pl.Buffered(1))


def _out_ln_kernel(x_ref, a_ref, r_ref, w_ref, g_ref, b_ref, o_ref, wb_ref, *, alpha, row_chunks):
    @pl.when(pl.program_id(0) == 0)
    def _():
        wb_ref[...] = w_ref[...].astype(BF16)

    rows = x_ref.shape[0] // row_chunks
    ka = a_ref.shape[1]
    for c in range(row_chunks):
        sl = slice(c * rows, (c + 1) * rows)
        mix = (jnp.dot(a_ref[sl, :], wb_ref[:ka, :], preferred_element_type=F32)
               + jnp.dot(r_ref[sl, :], wb_ref[ka:, :], preferred_element_type=F32))
        o_ref[sl, :] = _layer_norm_rows(alpha * x_ref[sl, :] + mix, g_ref[...], b_ref[...])


def _out_ln(x, attn, rec, w, g, b, *, alpha, tm=512, row_chunks=2):
    m, d = x.shape
    ka, kr = attn.shape[1], rec.shape[1]
    assert m % tm == 0 and tm % row_chunks == 0 and w.shape == (ka + kr, d)
    vmem = (2 * 2 * tm * d * 4 + 2 * tm * (ka + kr) * 2 + (ka + kr) * d * (4 + 2) + 3 * tm * d * 4 + (4 << 20))
    return pl.pallas_call(
        functools.partial(_out_ln_kernel, alpha=alpha, row_chunks=row_chunks),
        grid=(m // tm,),
        in_specs=[
            pl.BlockSpec((tm, d), lambda i: (i, 0)),
            pl.BlockSpec((tm, ka), lambda i: (i, 0)),
            pl.BlockSpec((tm, kr), lambda i: (i, 0)),
            _resident((ka + kr, d)), _resident((1, d)), _resident((1, d)),
        ],
        out_specs=pl.BlockSpec((tm, d), lambda i: (i, 0)),
        out_shape=jax.ShapeDtypeStruct((m, d), F32),
        scratch_shapes=[pltpu.VMEM((ka + kr, d), BF16)],
        compiler_params=_compiler_params(("arbitrary",), vmem),
        name="out_ln",
    )(x, attn, rec, w, g, b)


def _ple_kernel(x_ref, p_ref, wg_ref, wp_ref, o_ref, wgb_ref, wpb_ref):
    @pl.when(pl.program_id(0) == 0)
    def _():
        wgb_ref[...] = wg_ref[...].astype(BF16)
        wpb_ref[...] = wp_ref[...].astype(BF16)

    x = x_ref[...]
    gate = jax.nn.sigmoid(jnp.dot(x.astype(BF16), wgb_ref[...], preferred_element_type=F32))
    emb = jnp.dot(p_ref[...].astype(BF16), wpb_ref[...], preferred_element_type=F32)
    o_ref[...] = x + gate * emb


def _ple(x, p, wg, wp, *, tm=512):
    m, d = x.shape
    dp = p.shape[1]
    assert m % tm == 0
    vmem = 2 * 2 * tm * d * 4 + 2 * tm * dp * 4 + (d + dp) * d * (4 + 2) + 3 * tm * d * 4 + (4 << 20)
    return pl.pallas_call(
        _ple_kernel,
        grid=(m // tm,),
        in_specs=[
            pl.BlockSpec((tm, d), lambda i: (i, 0)),
            pl.BlockSpec((tm, dp), lambda i: (i, 0)),
            _resident((d, d)), _resident((dp, d)),
        ],
        out_specs=pl.BlockSpec((tm, d), lambda i: (i, 0)),
        out_shape=jax.ShapeDtypeStruct((m, d), F32),
        scratch_shapes=[pltpu.VMEM((d, d), BF16), pltpu.VMEM((dp, d), BF16)],
        compiler_params=_compiler_params(("arbitrary",), vmem),
        name="ple",
    )(x, p, wg, wp)


def kernel(x, p, positions, ffn1_w_gate, ffn1_w_up, ffn1_w_down, ln1_g, ln1_b, w_in, conv_w, conv_b, w_rgate, b_rgate, w_igate, b_igate, lru_lambda, w_out, ln2_g, ln2_b, ffn2_w_gate, ffn2_w_up, ffn2_w_down, ln3_g, ln3_b, w_ple_proj, w_ple_gate):
    batch, seq, d_model = x.shape
    depth = ffn1_w_gate.shape[0]
    m = batch * seq
    kv_width = N_KV_HEADS * HEAD_DIM
    qkv_width = (len(DILATIONS) + 2) * kv_width
    lru_width = d_model - kv_width
    alpha = (2.0 * depth) ** 0.25
    assert w_in.shape[2] == qkv_width + 2 * lru_width

    half = ROT_DIMS // 2
    inv_freq = jnp.power(jnp.float32(ROPE_THETA), -jnp.arange(half, dtype=F32) * (2.0 / ROT_DIMS))
    invf = jnp.zeros((1, V7X_LANES), F32).at[0, :ROT_DIMS].set(jnp.tile(inv_freq, 2))
    pos = positions.reshape(m, 1)
    row = lambda v: v.reshape(1, -1)

    h = x.reshape(m, d_model)
    for i in range(depth):
        h, (*ffn2_w, w_in_bf16) = _ffn_ln(
            h, ffn1_w_gate[i].astype(BF16), ffn1_w_up[i].astype(BF16), ffn1_w_down[i].astype(BF16),
            row(ln1_g[i]), row(ln1_b[i]), alpha=alpha, layer=i,
            cast_payload=(ffn2_w_gate, ffn2_w_up, ffn2_w_down, w_in))
        *qkv, hb = _qkv_proj(h, pos, invf, w_in_bf16, n=qkv_width)
        attn = _attention(qkv, batch=batch, seq=seq)
        w_ri = jnp.concatenate([w_rgate[i], w_igate[i]], axis=-1).astype(BF16)
        rec = _rg_lru(hb, w_in[i], conv_w[i], row(conv_b[i]), w_ri, row(b_rgate[i]), row(b_igate[i]),
                      row(lru_lambda[i]), batch=batch, seq=seq, x_col=qkv_width, width=lru_width)
        h = _out_ln(h, attn, rec, w_out[i], row(ln2_g[i]), row(ln2_b[i]), alpha=alpha)
        h, _ = _ffn_ln(h, *ffn2_w, row(ln3_g[i]), row(ln3_b[i]), alpha=alpha)
        h = _ple(h, p[i].reshape(m, -1), w_ple_gate[i], w_ple_proj[i])
    return h.reshape(batch, seq, d_model)
```

```python
import functools

import jax
import jax.numpy as jnp
from jax import lax
from jax.experimental import pallas as pl
from jax.experimental.pallas import tpu as pltpu

F32 = jnp.float32
BF16 = jnp.bfloat16

HEAD_DIM = 128
N_KV_HEADS = 4
DILATIONS = (1, 4, 16)
ATTN_SPAN = 128
ROT_DIMS = HEAD_DIM // 4
ROPE_THETA = 500000.0
LRU_BLOCK_WIDTH = 128
CONV_WIDTH = 4
LRU_C = 8.0
LN_EPS = 1e-5

V7X_LANES = 128
V7X_SUBLANES = 8
V7X_VMEM_SCOPED_BYTES = 60000 * 1024

ATTN_TILE = ATTN_SPAN * DILATIONS[-1]
ATTN_UNITS = ATTN_TILE // ATTN_SPAN
QKV_ROWS = ATTN_SPAN * DILATIONS[1]


def _compiler_params(semantics, vmem_estimate_bytes):
    limit = min(int(vmem_estimate_bytes), V7X_VMEM_SCOPED_BYTES)
    return pltpu.CompilerParams(dimension_semantics=semantics, vmem_limit_bytes=limit)


def _layer_norm_rows(y, g, b):
    mu = jnp.mean(y, axis=-1, keepdims=True)
    yc = y - mu
    var = jnp.mean(yc * yc, axis=-1, keepdims=True)
    return yc * lax.rsqrt(var + LN_EPS) * g + b


def _ffn_ln_kernel(x_ref, wg_ref, wu_ref, wd_ref, g_ref, b_ref, *rest, alpha, ln_row_chunks, n_cast):
    cast_in, o_ref, cast_out = rest[:n_cast], rest[n_cast], rest[n_cast + 1:]
    f = pl.program_id(1)
    last = pl.num_programs(1) - 1

    def cast_payload():
        for src, dst in zip(cast_in, cast_out):
            dst[...] = src[...].astype(BF16)

    def partial_sum(rows):
        xb = x_ref[rows, :].astype(BF16)
        gate = jnp.dot(xb, wg_ref[...], preferred_element_type=F32)
        up = jnp.dot(xb, wu_ref[...], preferred_element_type=F32)
        act = (jax.nn.silu(gate) * up).astype(BF16)
        return jnp.dot(act, wd_ref[...], preferred_element_type=F32)

    def accumulated(rows):
        return o_ref[rows, :] + partial_sum(rows)

    @pl.when(f == 0)
    def _():
        cast_payload()
        o_ref[...] = partial_sum(slice(None))

    @pl.when(jnp.logical_and(f > 0, f < last))
    def _():
        cast_payload()
        o_ref[...] = accumulated(slice(None))

    @pl.when(f == last)
    def _():
        cast_payload()
        rows_per = x_ref.shape[0] // ln_row_chunks
        for c in range(ln_row_chunks):
            rows = slice(c * rows_per, (c + 1) * rows_per)
            y = alpha * x_ref[rows, :] + 0.5 * accumulated(rows)
            o_ref[rows, :] = _layer_norm_rows(y, g_ref[...], b_ref[...])


def _ffn_ln(x, wg, wu, wd, g, b, *, alpha, cast_payload=(), layer=0, tm=1024, tf=512, ln_row_chunks=4):
    m, d = x.shape
    dff = wg.shape[1]
    assert m % tm == 0 and dff % tf == 0
    n_tiles, nf = m // tm, dff // tf
    pay_in, pay_out = [], []
    for a in cast_payload:
        _, rows, cols = a.shape
        if rows % n_tiles == 0 and cols % nf == 0:
            blk, idx = (rows // n_tiles, cols // nf), (lambda i, f: (i, f))
        else:
            blk, idx = (rows // nf, cols // n_tiles), (lambda i, f: (f, i))
        assert blk[0] * blk[1] * n_tiles * nf == rows * cols
        assert blk[0] % (2 * V7X_SUBLANES) == 0 and blk[1] % V7X_LANES == 0, blk
        pay_in.append(pl.BlockSpec((None,) + blk, lambda i, f, idx=idx: (layer,) + idx(i, f)))
        pay_out.append(pl.BlockSpec(blk, idx))
    vmem = (2 * tm * d * (4 + 4) + tm * d * 2 + 2 * 3 * d * tf * 2
            + 3 * tm * tf * 4 + (6 << 20)
            + sum(2 * s.block_shape[0] * s.block_shape[1] * (4 + 2) for s in pay_out))
    out, *cast = pl.pallas_call(
        functools.partial(_ffn_ln_kernel, alpha=alpha, ln_row_chunks=ln_row_chunks, n_cast=len(cast_payload)),
        grid=(n_tiles, nf),
        in_specs=[
            pl.BlockSpec((tm, d), lambda i, f: (i, 0)),
            pl.BlockSpec((d, tf), lambda i, f: (0, f)),
            pl.BlockSpec((d, tf), lambda i, f: (0, f)),
            pl.BlockSpec((tf, d), lambda i, f: (f, 0)),
            pl.BlockSpec((1, d), lambda i, f: (0, 0)),
            pl.BlockSpec((1, d), lambda i, f: (0, 0)),
        ] + pay_in,
        out_specs=[pl.BlockSpec((tm, d), lambda i, f: (i, 0))] + pay_out,
        out_shape=[jax.ShapeDtypeStruct((m, d), F32)]
        + [jax.ShapeDtypeStruct(a.shape[1:], BF16) for a in cast_payload],
        compiler_params=_compiler_params(("parallel", "arbitrary"), vmem),
        name="ffn_ln",
    )(x, wg, wu, wd, g, b, *cast_payload)
    return out, cast


def _qkv_proj_kernel(pos_ref, invf_ref, x_ref, w_ref,
                     q1_ref, q4_ref, q16_ref, k1_ref, k4_ref, k16_ref, v1_ref, v4_ref, v16_ref, xb_ref,
                     slab_ref):
    tm = x_ref.shape[0]
    heads = N_KV_HEADS
    half = ROT_DIMS // 2
    kvw = heads * HEAD_DIM
    x = x_ref[...].astype(BF16)
    xb_ref[...] = x

    ang = pos_ref[...].astype(F32) * invf_ref[...]
    lane = lax.broadcasted_iota(jnp.int32, ang.shape, 1)
    sin = jnp.sin(ang)
    cos = jnp.cos(ang)
    sin_lo = jnp.where(lane < half, -sin, 0.0)
    sin_hi = jnp.where(lane >= half, sin, 0.0)

    def rotate(t):
        return t * cos + pltpu.roll(t, HEAD_DIM - half, 1) * sin_lo + pltpu.roll(t, half, 1) * sin_hi

    def write_units(t, h, slab, o1_ref, o4_ref, o16_ref):
        if o1_ref is not None:
            for jb in range(tm // ATTN_SPAN):
                o1_ref[jb, h] = t[jb * ATTN_SPAN:(jb + 1) * ATTN_SPAN].astype(BF16)
        if o4_ref is None and o16_ref is None:
            return
        slab_ref[slab] = t
        for o_ref, d in ((o4_ref, DILATIONS[1]), (o16_ref, DILATIONS[2])):
            if o_ref is not None:
                for r in range(d):
                    o_ref[r, h] = slab_ref[slab, pl.ds(r, tm // d, stride=d), :].astype(BF16)

    groups = (
        (0, True, (q1_ref, None, None)), (1, True, (None, q4_ref, None)), (2, True, (None, None, q16_ref)),
        (3, True, (k1_ref, k4_ref, k16_ref)), (4, False, (v1_ref, v4_ref, v16_ref)),
    )
    for step, (j, rot, outs) in enumerate(groups):
        acc = jnp.dot(x, w_ref[:, j * kvw:(j + 1) * kvw], preferred_element_type=F32)
        for h in range(heads):
            t = acc[:, h * HEAD_DIM:(h + 1) * HEAD_DIM]
            write_units(rotate(t) if rot else t, h, (step * heads + h) % slab_ref.shape[0], *outs)


def _qkv_proj(x, pos, invf, w, *, n):
    m, d = x.shape
    tm = QKV_ROWS
    heads = N_KV_HEADS
    per_tile = ATTN_TILE // tm
    assert m % ATTN_TILE == 0 and n == 5 * heads * HEAD_DIM and w.shape[1] >= n
    units = lambda rows: jax.ShapeDtypeStruct((rows // ATTN_SPAN, heads, ATTN_SPAN, HEAD_DIM), BF16)
    flat_spec = pl.BlockSpec((tm // ATTN_SPAN, heads, ATTN_SPAN, HEAD_DIM), lambda i: (i, 0, 0, 0))
    wide_spec = pl.BlockSpec((DILATIONS[2], heads, tm // DILATIONS[2], HEAD_DIM),
                             lambda i: (i // per_tile, 0, i % per_tile, 0))
    out_specs = [flat_spec, flat_spec, wide_spec] * 3 + [pl.BlockSpec((tm, d), lambda i: (i, 0))]
    n_slabs = 4
    vmem = (2 * tm * d * 4 + tm * d * 2 + 2 * d * n * 2 + 2 * 9 * tm * heads * HEAD_DIM * 2 + 2 * tm * d * 2
            + n_slabs * tm * HEAD_DIM * 4 + 8 * tm * heads * HEAD_DIM * 4 + (6 << 20))
    return pl.pallas_call(
        _qkv_proj_kernel,
        grid=(m // tm,),
        in_specs=[
            pl.BlockSpec((tm, 1), lambda i: (i, 0)),
            pl.BlockSpec((1, V7X_LANES), lambda i: (0, 0)),
            pl.BlockSpec((tm, d), lambda i: (i, 0)),
            pl.BlockSpec((d, n), lambda i: (0, 0)),
        ],
        out_specs=out_specs,
        out_shape=[units(m)] * 9 + [jax.ShapeDtypeStruct((m, d), BF16)],
        scratch_shapes=[pltpu.VMEM((n_slabs, tm, HEAD_DIM), F32)],
        compiler_params=_compiler_params(("parallel",), vmem),
        name="qkv_proj",
    )(pos, invf, x, w)


def _attn_kernel(q1_ref, q4_ref, q16_ref, k1c_ref, k4c_ref, k16c_ref, k1p_ref, k4p_ref, k16p_ref,
                 v1c_ref, v4c_ref, v16c_ref, v1p_ref, v4p_ref, v16p_ref, o_ref, acc_ref, lse_ref):
    first_tile = pl.program_id(2) == 0
    blk = ATTN_SPAN
    nu = ATTN_UNITS
    scale = HEAD_DIM ** -0.5
    qi = lax.broadcasted_iota(jnp.int32, (1, blk, blk), 1)
    ki = lax.broadcasted_iota(jnp.int32, (1, blk, blk), 2)
    unit = lax.broadcasted_iota(jnp.int32, (nu, 1, 1), 0)
    neg_inf = -jnp.inf
    qk = lambda a, b: jnp.einsum("uqd,ukd->uqk", a, b, preferred_element_type=F32)
    pv = lambda a, b: jnp.einsum("uqk,ukd->uqd", a, b, preferred_element_type=F32)

    def with_prev(fn, lhs, prev_ref, cur_ref, d, hh):
        head = fn(lhs[:d], prev_ref[:, hh])
        if d == nu:
            return head
        return jnp.concatenate([head, fn(lhs[d:], cur_ref[0:nu - d, hh])], axis=0)

    refs = ((q1_ref, k1c_ref, k1p_ref, v1c_ref, v1p_ref), (q4_ref, k4c_ref, k4p_ref, v4c_ref, v4p_ref),
            (q16_ref, k16c_ref, k16p_ref, v16c_ref, v16p_ref))
    for hh in range(q1_ref.shape[1]):
        for p, (d, (q_ref, kc_ref, kp_ref, vc_ref, vp_ref)) in enumerate(zip(DILATIONS, refs)):
            q = q_ref[:, hh]
            sp = with_prev(qk, q, kp_ref, kc_ref, d, hh) * scale
            sc = qk(q, kc_ref[:, hh]) * scale
            sp = jnp.where(ki >= qi, sp, neg_inf)
            sp = jnp.where(jnp.logical_and(first_tile, unit < d), neg_inf, sp)
            sc = jnp.where(ki <= qi, sc, neg_inf)
            mx = jnp.max(jnp.maximum(sp, sc), axis=-1, keepdims=True)
            ep = jnp.exp(sp - mx)
            ec = jnp.exp(sc - mx)
            den = jnp.sum(ep + ec, axis=-1, keepdims=True)
            out = (with_prev(pv, ep.astype(BF16), vp_ref, vc_ref, d, hh) + pv(ec.astype(BF16), vc_ref[:, hh])) / den
            lse = jnp.broadcast_to(mx + jnp.log(den), out.shape)
            if d == 1:
                acc_ref[hh, p] = out.reshape(ATTN_TILE, HEAD_DIM)
                lse_ref[hh, p] = lse.reshape(ATTN_TILE, HEAD_DIM)
            else:
                for u in range(nu):
                    jb, r = divmod(u, d) if d < nu else (0, u)
                    rows = pl.ds(d * blk * jb + r, blk, stride=d)
                    acc_ref[hh, p, rows, :] = out[u]
                    lse_ref[hh, p, rows, :] = lse[u]

        lse = lse_ref[hh]
        w = jnp.exp(lse - jnp.max(lse, axis=0, keepdims=True))
        merged = jnp.sum(w * acc_ref[hh], axis=0) / jnp.sum(w, axis=0)
        o_ref[:, hh * HEAD_DIM:(hh + 1) * HEAD_DIM] = merged.astype(o_ref.dtype)


def _attention(qkv, *, batch, seq, heads_per_step=2):
    nt = seq // ATTN_TILE
    heads = N_KV_HEADS
    nu = ATTN_UNITS
    hps = heads_per_step
    assert heads % hps == 0
    q1, q4, q16, k1, k4, k16, v1, v4, v16 = qkv

    def cur():
        return pl.BlockSpec((nu, hps, ATTN_SPAN, HEAD_DIM), lambda b, h, n: (b * nt + n, h, 0, 0))

    def prev(d):
        per = nu // d
        return pl.BlockSpec((d, hps, ATTN_SPAN, HEAD_DIM),
                            lambda b, h, n: (jnp.maximum((b * nt + n) * per - 1, 0), h, 0, 0))

    unit_bytes = ATTN_SPAN * HEAD_DIM * 2
    tile_f32 = ATTN_TILE * HEAD_DIM * 4
    vmem = (hps * (2 * (9 * nu + 2 * sum(DILATIONS)) * unit_bytes + 2 * tile_f32 // 2 + 6 * tile_f32)
            + 12 * tile_f32 + (4 << 20))
    return pl.pallas_call(
        _attn_kernel,
        grid=(batch, heads // hps, nt),
        in_specs=[cur(), cur(), cur(), cur(), cur(), cur(), prev(1), prev(4), prev(16),
                  cur(), cur(), cur(), prev(1), prev(4), prev(16)],
        out_specs=pl.BlockSpec((ATTN_TILE, hps * HEAD_DIM), lambda b, h, n: (b * nt + n, h)),
        out_shape=jax.ShapeDtypeStruct((batch * seq, heads * HEAD_DIM), BF16),
        scratch_shapes=[pltpu.VMEM((hps, len(DILATIONS), ATTN_TILE, HEAD_DIM), F32)] * 2,
        compiler_params=_compiler_params(("parallel", "parallel", "arbitrary"), vmem),
        name="dilated_attention",
    )(q1, q4, q16, k1, k4, k16, k1, k4, k16, v1, v4, v16, v1, v4, v16)


SCAN_PAD_ROWS = V7X_SUBLANES


def _linear_scan_rows(a, u, h0, sa_ref, su_ref):
    rows, width = a.shape
    sub = V7X_SUBLANES
    run = rows // sub
    pitch = run + SCAN_PAD_ROWS
    lanes = V7X_LANES
    out = []
    carry = []
    for c in range(width // lanes):
        cols = slice(c * lanes, (c + 1) * lanes)
        for s in range(sub):
            sa_ref[c, s * pitch:s * pitch + run, :] = a[s * run:(s + 1) * run, cols]
            su_ref[c, s * pitch:s * pitch + run, :] = u[s * run:(s + 1) * run, cols]
        h = jnp.zeros((sub, lanes), F32)
        prod = jnp.ones((sub, lanes), F32)
        hs, prods = [], []
        for k in range(run):
            ak = sa_ref[c, pl.ds(k, sub, stride=pitch), :]
            uk = su_ref[c, pl.ds(k, sub, stride=pitch), :]
            h = ak * h + uk
            prod = ak * prod
            hs.append(h)
            prods.append(prod)
        start = h0[:, cols]
        enter = [start]
        for s in range(sub - 1):
            start = h[s:s + 1, :] + prod[s:s + 1, :] * start
            enter.append(start)
        carry.append(h[sub - 1:sub, :] + prod[sub - 1:sub, :] * start)
        enter = jnp.concatenate(enter, axis=0)
        for k in range(run):
            su_ref[c, pl.ds(k, sub, stride=pitch), :] = hs[k] + prods[k] * enter
        out.append(jnp.concatenate([su_ref[c, s * pitch:s * pitch + run, :] for s in range(sub)], axis=0))
    return jnp.concatenate(out, axis=1), jnp.concatenate(carry, axis=1)


def _gelu_tanh(y):
    z = 0.7978845608028654 * (y + 0.044715 * (y * y * y))
    return y * jax.nn.sigmoid(2.0 * z)


def _lru_kernel(hin_ref, *refs, nc):
    wx_refs, wy_refs = refs[:nc], refs[nc:2 * nc]
    cw_ref, cb_ref, wri_ref, br_ref, bi_ref, lam_ref, o_ref, xs_ref, ys_ref, h_ref, sa_ref, su_ref = refs[2 * nc:]
    tl = hin_ref.shape[0]
    tc = wx_refs[0].shape[1]
    pad = V7X_SUBLANES
    s = pl.program_id(1)

    @pl.when(s == 0)
    def _():
        xs_ref[...] = jnp.zeros_like(xs_ref)
        ys_ref[...] = jnp.zeros_like(ys_ref)

    @pl.when(s <= 1)
    def _():
        xs_ref[:, 0:pad, :] = jnp.zeros((nc, pad, tc), F32)
        h_ref[...] = jnp.zeros_like(h_ref)

    half = tc // 2
    bw = LRU_BLOCK_WIDTH
    gpc = tc // bw
    for c in range(nc):
        ch = slice(c * tc, (c + 1) * tc)

        def project(dst_ref, rows, w_ref, part):
            cols = slice(part * half, (part + 1) * half)
            dst_ref[c, rows, cols] = jnp.dot(hin_ref[...], w_ref[:, cols], preferred_element_type=F32)

        x = xs_ref[c, pad:pad + tl, :]
        y = ys_ref[c]
        project(ys_ref, slice(0, tl), wy_refs[c], 0)
        xc = cw_ref[CONV_WIDTH - 1:CONV_WIDTH, ch] * x
        for j in range(CONV_WIDTH - 1):
            shift = CONV_WIDTH - 1 - j
            xc = xc + cw_ref[j:j + 1, ch] * xs_ref[c, pad - shift:pad - shift + tl, :]
        xc = xc + cb_ref[:, ch]
        xs_ref[c, 0:pad, :] = x[tl - pad:tl]

        pre = [jnp.dot(xc[:, g * bw:(g + 1) * bw].astype(BF16), wri_ref[c * gpc + g], preferred_element_type=F32)
               for g in range(gpc)]
        project(ys_ref, slice(0, tl), wy_refs[c], 1)
        project(xs_ref, slice(pad, pad + tl), wx_refs[c], 0)

        r = jax.nn.sigmoid(jnp.concatenate([t[:, :bw] for t in pre], axis=1) + br_ref[:, ch])
        i = jax.nn.sigmoid(jnp.concatenate([t[:, bw:] for t in pre], axis=1) + bi_ref[:, ch])
        neg_lam = -lam_ref[:, ch]
        softplus = jnp.maximum(neg_lam, 0.0) + jnp.log1p(jnp.exp(-jnp.abs(neg_lam)))
        log_a = (-LRU_C * softplus) * r
        a = jnp.exp(log_a)
        one_m_a2 = 1.0 - a * a
        root = jnp.where(one_m_a2 > 0.0, one_m_a2 * lax.rsqrt(one_m_a2), 0.0)
        u = root * (i * xc)
        project(xs_ref, slice(pad, pad + tl), wx_refs[c], 1)
        h, h_ref[c] = _linear_scan_rows(a, u, h_ref[c], sa_ref.at[c], su_ref.at[c])
        o_ref[:, ch] = (h * _gelu_tanh(y)).astype(o_ref.dtype)


def _rg_lru(hin, w_in, conv_w, conv_b, w_ri, b_r, b_i, lam, *, batch, seq, x_col, width, tl=256, tc=512):
    m, d = hin.shape
    nt = seq // tl
    nc = width // tc
    assert seq % tl == 0 and width % tc == 0 and x_col % tc == 0
    whole = lambda shape: pl.BlockSpec(shape, lambda b, n: (0,) * len(shape), pipeline_mode=pl.Buffered(1))
    w_block = lambda col: pl.BlockSpec((d, tc), lambda b, n: (0, col), pipeline_mode=pl.Buffered(1))
    vmem = (2 * tl * d * 2 + 2 * nc * d * tc * 2 + 2 * tl * width * 2 + width * 256 * 2
            + (4 * tl + 6 * V7X_SUBLANES * SCAN_PAD_ROWS) * width * 4 + 16 * tl * tc * 4 + (4 << 20))
    return pl.pallas_call(
        functools.partial(_lru_kernel, nc=nc),
        grid=(batch, nt + 1),
        in_specs=[pl.BlockSpec((tl, d), lambda b, n: (b * nt + jnp.minimum(n, nt - 1), 0))]
        + [w_block(x_col // tc + c) for c in range(2 * nc)]
        + [whole((CONV_WIDTH, width)), whole((1, width)),
           whole((width // LRU_BLOCK_WIDTH, LRU_BLOCK_WIDTH, 2 * LRU_BLOCK_WIDTH)),
           whole((1, width)), whole((1, width)), whole((1, width))],
        out_specs=pl.BlockSpec((tl, width), lambda b, n: (b * nt + jnp.maximum(n - 1, 0), 0)),
        out_shape=jax.ShapeDtypeStruct((m, width), BF16),
        scratch_shapes=[pltpu.VMEM((nc, tl + V7X_SUBLANES, tc), F32), pltpu.VMEM((nc, tl, tc), F32),
                        pltpu.VMEM((nc, 1, tc), F32)]
        + [pltpu.VMEM((nc, tc // V7X_LANES, tl + V7X_SUBLANES * SCAN_PAD_ROWS, V7X_LANES), F32)] * 2,
        compiler_params=_compiler_params(("parallel", "arbitrary"), vmem),
        name="rg_lru",
    )(hin, *([w_in] * (2 * nc)), conv_w, conv_b, w_ri, b_r, b_i, lam)


def _resident(shape):
    return pl.BlockSpec(shape, lambda i: (0,) * len(shape), pipeline_mode=pl.Buffered(1))


def _out_ln_kernel(x_ref, a_ref, r_ref, w_ref, g_ref, b_ref, o_ref, wb_ref, *, alpha, row_chunks):
    @pl.when(pl.program_id(0) == 0)
    def _():
        wb_ref[...] = w_ref[...].astype(BF16)

    rows = x_ref.shape[0] // row_chunks
    ka = a_ref.shape[1]
    for c in range(row_chunks):
        sl = slice(c * rows, (c + 1) * rows)
        mix = (jnp.dot(a_ref[sl, :], wb_ref[:ka, :], preferred_element_type=F32)
               + jnp.dot(r_ref[sl, :], wb_ref[ka:, :], preferred_element_type=F32))
        o_ref[sl, :] = _layer_norm_rows(alpha * x_ref[sl, :] + mix, g_ref[...], b_ref[...])


def _out_ln(x, attn, rec, w, g, b, *, alpha, tm=512, row_chunks=2):
    m, d = x.shape
    ka, kr = attn.shape[1], rec.shape[1]
    assert m % tm == 0 and tm % row_chunks == 0 and w.shape == (ka + kr, d)
    vmem = (2 * 2 * tm * d * 4 + 2 * tm * (ka + kr) * 2 + (ka + kr) * d * (4 + 2) + 3 * tm * d * 4 + (4 << 20))
    return pl.pallas_call(
        functools.partial(_out_ln_kernel, alpha=alpha, row_chunks=row_chunks),
        grid=(m // tm,),
        in_specs=[
            pl.BlockSpec((tm, d), lambda i: (i, 0)),
            pl.BlockSpec((tm, ka), lambda i: (i, 0)),
            pl.BlockSpec((tm, kr), lambda i: (i, 0)),
            _resident((ka + kr, d)), _resident((1, d)), _resident((1, d)),
        ],
        out_specs=pl.BlockSpec((tm, d), lambda i: (i, 0)),
        out_shape=jax.ShapeDtypeStruct((m, d), F32),
        scratch_shapes=[pltpu.VMEM((ka + kr, d), BF16)],
        compiler_params=_compiler_params(("arbitrary",), vmem),
        name="out_ln",
    )(x, attn, rec, w, g, b)


def _ple_kernel(x_ref, p_ref, wg_ref, wp_ref, o_ref, wgb_ref, wpb_ref):
    @pl.when(pl.program_id(0) == 0)
    def _():
        wgb_ref[...] = wg_ref[...].astype(BF16)
        wpb_ref[...] = wp_ref[...].astype(BF16)

    x = x_ref[...]
    gate = jax.nn.sigmoid(jnp.dot(x.astype(BF16), wgb_ref[...], preferred_element_type=F32))
    emb = jnp.dot(p_ref[...].astype(BF16), wpb_ref[...], preferred_element_type=F32)
    o_ref[...] = x + gate * emb


def _ple(x, p, wg, wp, *, tm=512):
    m, d = x.shape
    dp = p.shape[1]
    assert m % tm == 0
    vmem = 2 * 2 * tm * d * 4 + 2 * tm * dp * 4 + (d + dp) * d * (4 + 2) + 3 * tm * d * 4 + (4 << 20)
    return pl.pallas_call(
        _ple_kernel,
        grid=(m // tm,),
        in_specs=[
            pl.BlockSpec((tm, d), lambda i: (i, 0)),
            pl.BlockSpec((tm, dp), lambda i: (i, 0)),
            _resident((d, d)), _resident((dp, d)),
        ],
        out_specs=pl.BlockSpec((tm, d), lambda i: (i, 0)),
        out_shape=jax.ShapeDtypeStruct((m, d), F32),
        scratch_shapes=[pltpu.VMEM((d, d), BF16), pltpu.VMEM((dp, d), BF16)],
        compiler_params=_compiler_params(("arbitrary",), vmem),
        name="ple",
    )(x, p, wg, wp)


def kernel(x, p, positions, ffn1_w_gate, ffn1_w_up, ffn1_w_down, ln1_g, ln1_b, w_in, conv_w, conv_b, w_rgate, b_rgate, w_igate, b_igate, lru_lambda, w_out, ln2_g, ln2_b, ffn2_w_gate, ffn2_w_up, ffn2_w_down, ln3_g, ln3_b, w_ple_proj, w_ple_gate):
    batch, seq, d_model = x.shape
    depth = ffn1_w_gate.shape[0]
    m = batch * seq
    kv_width = N_KV_HEADS * HEAD_DIM
    qkv_width = (len(DILATIONS) + 2) * kv_width
    lru_width = d_model - kv_width
    alpha = (2.0 * depth) ** 0.25
    assert w_in.shape[2] == qkv_width + 2 * lru_width

    half = ROT_DIMS // 2
    inv_freq = jnp.power(jnp.float32(ROPE_THETA), -jnp.arange(half, dtype=F32) * (2.0 / ROT_DIMS))
    invf = jnp.zeros((1, V7X_LANES), F32).at[0, :ROT_DIMS].set(jnp.tile(inv_freq, 2))
    pos = positions.reshape(m, 1)
    row = lambda v: v.reshape(1, -1)

    h = x.reshape(m, d_model)
    for i in range(depth):
        h, (*ffn2_w, w_in_bf16) = _ffn_ln(
            h, ffn1_w_gate[i].astype(BF16), ffn1_w_up[i].astype(BF16), ffn1_w_down[i].astype(BF16),
            row(ln1_g[i]), row(ln1_b[i]), alpha=alpha, layer=i,
            cast_payload=(ffn2_w_gate, ffn2_w_up, ffn2_w_down, w_in))
        *qkv, hb = _qkv_proj(h, pos, invf, w_in_bf16, n=qkv_width)
        attn = _attention(qkv, batch=batch, seq=seq)
        w_ri = jnp.concatenate([w_rgate[i], w_igate[i]], axis=-1).astype(BF16)
        rec = _rg_lru(hb, w_in_bf16, conv_w[i], row(conv_b[i]), w_ri, row(b_rgate[i]), row(b_igate[i]),
                      row(lru_lambda[i]), batch=batch, seq=seq, x_col=qkv_width, width=lru_width)
        h = _out_ln(h, attn, rec, w_out[i], row(ln2_g[i]), row(ln2_b[i]), alpha=alpha)
        h, _ = _ffn_ln(h, *ffn2_w, row(ln3_g[i]), row(ln3_b[i]), alpha=alpha)
        h = _ple(h, p[i].reshape(m, -1), w_ple_gate[i], w_ple_proj[i])
    return h.reshape(batch, seq, d_model)
```

```python
import functools

import jax
import jax.numpy as jnp
from jax import lax
from jax.experimental import pallas as pl
from jax.experimental.pallas import tpu as pltpu

F32 = jnp.float32
BF16 = jnp.bfloat16

HEAD_DIM = 128
N_KV_HEADS = 4
DILATIONS = (1, 4, 16)
ATTN_SPAN = 128
ROT_DIMS = HEAD_DIM // 4
ROPE_THETA = 500000.0
LRU_BLOCK_WIDTH = 128
CONV_WIDTH = 4
LRU_C = 8.0
LN_EPS = 1e-5

V7X_LANES = 128
V7X_SUBLANES = 8
V7X_VMEM_SCOPED_BYTES = 60000 * 1024

ATTN_TILE = ATTN_SPAN * DILATIONS[-1]
ATTN_UNITS = ATTN_TILE // ATTN_SPAN
QKV_ROWS = ATTN_SPAN * DILATIONS[1]


def _compiler_params(semantics, vmem_estimate_bytes):
    limit = min(int(vmem_estimate_bytes), V7X_VMEM_SCOPED_BYTES)
    return pltpu.CompilerParams(dimension_semantics=semantics, vmem_limit_bytes=limit)


def _layer_norm_rows(y, g, b):
    mu = jnp.mean(y, axis=-1, keepdims=True)
    yc = y - mu
    var = jnp.mean(yc * yc, axis=-1, keepdims=True)
    return yc * lax.rsqrt(var + LN_EPS) * g + b


def _ffn_ln_kernel(x_ref, wg_ref, wu_ref, wd_ref, g_ref, b_ref, *rest, alpha, ln_row_chunks, n_cast):
    cast_in, o_ref, cast_out = rest[:n_cast], rest[n_cast], rest[n_cast + 1:]
    f = pl.program_id(1)
    last = pl.num_programs(1) - 1

    def cast_payload():
        for src, dst in zip(cast_in, cast_out):
            dst[...] = src[...].astype(BF16)

    def partial_sum(rows):
        xb = x_ref[rows, :].astype(BF16)
        gate = jnp.dot(xb, wg_ref[...], preferred_element_type=F32)
        up = jnp.dot(xb, wu_ref[...], preferred_element_type=F32)
        act = (jax.nn.silu(gate) * up).astype(BF16)
        return jnp.dot(act, wd_ref[...], preferred_element_type=F32)

    def accumulated(rows):
        return o_ref[rows, :] + partial_sum(rows)

    @pl.when(f == 0)
    def _():
        cast_payload()
        o_ref[...] = partial_sum(slice(None))

    @pl.when(jnp.logical_and(f > 0, f < last))
    def _():
        cast_payload()
        o_ref[...] = accumulated(slice(None))

    @pl.when(f == last)
    def _():
        cast_payload()
        rows_per = x_ref.shape[0] // ln_row_chunks
        for c in range(ln_row_chunks):
            rows = slice(c * rows_per, (c + 1) * rows_per)
            y = alpha * x_ref[rows, :] + 0.5 * accumulated(rows)
            o_ref[rows, :] = _layer_norm_rows(y, g_ref[...], b_ref[...])


def _ffn_ln(x, wg, wu, wd, g, b, *, alpha, cast_payload=(), layer=0, tm=1024, tf=512, ln_row_chunks=4):
    m, d = x.shape
    dff = wg.shape[1]
    assert m % tm == 0 and dff % tf == 0
    n_tiles, nf = m // tm, dff // tf
    pay_in, pay_out = [], []
    for a in cast_payload:
        _, rows, cols = a.shape
        if rows % (n_tiles * nf * 2 * V7X_SUBLANES) == 0:
            blk, idx = (rows // (n_tiles * nf), cols), (lambda i, f: (i * nf + f, 0))
        elif rows % n_tiles == 0 and cols % nf == 0:
            blk, idx = (rows // n_tiles, cols // nf), (lambda i, f: (i, f))
        else:
            blk, idx = (rows // nf, cols // n_tiles), (lambda i, f: (f, i))
        assert blk[0] * blk[1] * n_tiles * nf == rows * cols
        assert blk[0] % (2 * V7X_SUBLANES) == 0 and blk[1] % V7X_LANES == 0, blk
        pay_in.append(pl.BlockSpec((None,) + blk, lambda i, f, idx=idx: (layer,) + idx(i, f)))
        pay_out.append(pl.BlockSpec(blk, idx))
    vmem = (2 * tm * d * (4 + 4) + tm * d * 2 + 2 * 3 * d * tf * 2
            + 3 * tm * tf * 4 + (6 << 20)
            + sum(2 * s.block_shape[0] * s.block_shape[1] * (4 + 2) for s in pay_out))
    out, *cast = pl.pallas_call(
        functools.partial(_ffn_ln_kernel, alpha=alpha, ln_row_chunks=ln_row_chunks, n_cast=len(cast_payload)),
        grid=(n_tiles, nf),
        in_specs=[
            pl.BlockSpec((tm, d), lambda i, f: (i, 0)),
            pl.BlockSpec((d, tf), lambda i, f: (0, f)),
            pl.BlockSpec((d, tf), lambda i, f: (0, f)),
            pl.BlockSpec((tf, d), lambda i, f: (f, 0)),
            pl.BlockSpec((1, d), lambda i, f: (0, 0)),
            pl.BlockSpec((1, d), lambda i, f: (0, 0)),
        ] + pay_in,
        out_specs=[pl.BlockSpec((tm, d), lambda i, f: (i, 0))] + pay_out,
        out_shape=[jax.ShapeDtypeStruct((m, d), F32)]
        + [jax.ShapeDtypeStruct(a.shape[1:], BF16) for a in cast_payload],
        compiler_params=_compiler_params(("parallel", "arbitrary"), vmem),
        name="ffn_ln",
    )(x, wg, wu, wd, g, b, *cast_payload)
    return out, cast


def _qkv_proj_kernel(pos_ref, invf_ref, x_ref, w_ref,
                     q1_ref, q4_ref, q16_ref, k1_ref, k4_ref, k16_ref, v1_ref, v4_ref, v16_ref, xb_ref,
                     slab_ref):
    tm = x_ref.shape[0]
    heads = N_KV_HEADS
    half = ROT_DIMS // 2
    kvw = heads * HEAD_DIM
    x = x_ref[...].astype(BF16)
    xb_ref[...] = x

    ang = pos_ref[...].astype(F32) * invf_ref[...]
    lane = lax.broadcasted_iota(jnp.int32, ang.shape, 1)
    sin = jnp.sin(ang)
    cos = jnp.cos(ang)
    sin_lo = jnp.where(lane < half, -sin, 0.0)
    sin_hi = jnp.where(lane >= half, sin, 0.0)

    def rotate(t):
        return t * cos + pltpu.roll(t, HEAD_DIM - half, 1) * sin_lo + pltpu.roll(t, half, 1) * sin_hi

    def write_units(t, h, slab, o1_ref, o4_ref, o16_ref):
        if o1_ref is not None:
            for jb in range(tm // ATTN_SPAN):
                o1_ref[jb, h] = t[jb * ATTN_SPAN:(jb + 1) * ATTN_SPAN].astype(BF16)
        if o4_ref is None and o16_ref is None:
            return
        d4 = DILATIONS[1]
        slab_ref[0, slab] = t
        for r4 in range(d4):
            rows4 = slab_ref[0, slab, pl.ds(r4, tm // d4, stride=d4), :]
            if o4_ref is not None:
                o4_ref[r4, h] = rows4.astype(BF16)
            if o16_ref is not None:
                slab_ref[1, slab, r4 * (tm // d4):(r4 + 1) * (tm // d4), :] = rows4
        if o16_ref is not None:
            d16 = DILATIONS[2]
            for r16 in range(d16):
                r4, c = r16 % d4, r16 // d4
                start = r4 * (tm // d4) + c
                o16_ref[r16, h] = slab_ref[1, slab, pl.ds(start, tm // d16, stride=d16 // d4), :].astype(BF16)

    groups = (
        (0, True, (q1_ref, None, None)), (1, True, (None, q4_ref, None)), (2, True, (None, None, q16_ref)),
        (3, True, (k1_ref, k4_ref, k16_ref)), (4, False, (v1_ref, v4_ref, v16_ref)),
    )
    for step, (j, rot, outs) in enumerate(groups):
        acc = jnp.dot(x, w_ref[:, j * kvw:(j + 1) * kvw], preferred_element_type=F32)
        for h in range(heads):
            t = acc[:, h * HEAD_DIM:(h + 1) * HEAD_DIM]
            write_units(rotate(t) if rot else t, h, (step * heads + h) % slab_ref.shape[1], *outs)


def _qkv_proj(x, pos, invf, w, *, n):
    m, d = x.shape
    tm = QKV_ROWS
    heads = N_KV_HEADS
    per_tile = ATTN_TILE // tm
    assert m % ATTN_TILE == 0 and n == 5 * heads * HEAD_DIM and w.shape[1] >= n
    units = lambda rows: jax.ShapeDtypeStruct((rows // ATTN_SPAN, heads, ATTN_SPAN, HEAD_DIM), BF16)
    flat_spec = pl.BlockSpec((tm // ATTN_SPAN, heads, ATTN_SPAN, HEAD_DIM), lambda i: (i, 0, 0, 0))
    wide_spec = pl.BlockSpec((DILATIONS[2], heads, tm // DILATIONS[2], HEAD_DIM),
                             lambda i: (i // per_tile, 0, i % per_tile, 0))
    out_specs = [flat_spec, flat_spec, wide_spec] * 3 + [pl.BlockSpec((tm, d), lambda i: (i, 0))]
    n_slabs = 4
    vmem = (2 * tm * d * 4 + tm * d * 2 + 2 * d * n * 2 + 2 * 9 * tm * heads * HEAD_DIM * 2 + 2 * tm * d * 2
            + 2 * n_slabs * tm * HEAD_DIM * 4 + 8 * tm * heads * HEAD_DIM * 4 + (6 << 20))
    return pl.pallas_call(
        _qkv_proj_kernel,
        grid=(m // tm,),
        in_specs=[
            pl.BlockSpec((tm, 1), lambda i: (i, 0)),
            pl.BlockSpec((1, V7X_LANES), lambda i: (0, 0)),
            pl.BlockSpec((tm, d), lambda i: (i, 0)),
            pl.BlockSpec((d, n), lambda i: (0, 0)),
        ],
        out_specs=out_specs,
        out_shape=[units(m)] * 9 + [jax.ShapeDtypeStruct((m, d), BF16)],
        scratch_shapes=[pltpu.VMEM((2, n_slabs, tm, HEAD_DIM), F32)],
        compiler_params=_compiler_params(("parallel",), vmem),
        name="qkv_proj",
    )(pos, invf, x, w)


def _attn_kernel(q1_ref, q4_ref, q16_ref, k1c_ref, k4c_ref, k16c_ref, k1p_ref, k4p_ref, k16p_ref,
                 v1c_ref, v4c_ref, v16c_ref, v1p_ref, v4p_ref, v16p_ref, o_ref, acc_ref, lse_ref, stage_ref):
    first_tile = pl.program_id(2) == 0
    blk = ATTN_SPAN
    nu = ATTN_UNITS
    scale = HEAD_DIM ** -0.5
    qi = lax.broadcasted_iota(jnp.int32, (1, blk, blk), 1)
    ki = lax.broadcasted_iota(jnp.int32, (1, blk, blk), 2)
    unit = lax.broadcasted_iota(jnp.int32, (nu, 1, 1), 0)
    neg_inf = -jnp.inf
    qk = lambda a, b: jnp.einsum("uqd,ukd->uqk", a, b, preferred_element_type=F32)
    pv = lambda a, b: jnp.einsum("uqk,ukd->uqd", a, b, preferred_element_type=F32)

    def with_prev(fn, lhs, prev_ref, cur_ref, d, hh):
        head = fn(lhs[:d], prev_ref[:, hh])
        if d == nu:
            return head
        return jnp.concatenate([head, fn(lhs[d:], cur_ref[0:nu - d, hh])], axis=0)

    refs = ((q1_ref, k1c_ref, k1p_ref, v1c_ref, v1p_ref), (q4_ref, k4c_ref, k4p_ref, v4c_ref, v4p_ref),
            (q16_ref, k16c_ref, k16p_ref, v16c_ref, v16p_ref))
    for hh in range(q1_ref.shape[1]):
        for p, (d, (q_ref, kc_ref, kp_ref, vc_ref, vp_ref)) in enumerate(zip(DILATIONS, refs)):
            q = q_ref[:, hh]
            sp = with_prev(qk, q, kp_ref, kc_ref, d, hh) * scale
            sc = qk(q, kc_ref[:, hh]) * scale
            sp = jnp.where(ki >= qi, sp, neg_inf)
            sp = jnp.where(jnp.logical_and(first_tile, unit < d), neg_inf, sp)
            sc = jnp.where(ki <= qi, sc, neg_inf)
            mx = jnp.max(jnp.maximum(sp, sc), axis=-1, keepdims=True)
            ep = jnp.exp(sp - mx)
            ec = jnp.exp(sc - mx)
            den = jnp.sum(ep + ec, axis=-1, keepdims=True)
            out = (with_prev(pv, ep.astype(BF16), vp_ref, vc_ref, d, hh) + pv(ec.astype(BF16), vc_ref[:, hh])) / den
            lse = jnp.broadcast_to(mx + jnp.log(den), out.shape)
            d4 = DILATIONS[1]
            if d == 1:
                acc_ref[hh, p] = out.reshape(ATTN_TILE, HEAD_DIM)
                lse_ref[hh, p] = lse.reshape(ATTN_TILE, HEAD_DIM)
            elif d == d4:
                for u in range(nu):
                    jb, r = divmod(u, d)
                    rows = pl.ds(d * blk * jb + r, blk, stride=d)
                    acc_ref[hh, p, rows, :] = out[u]
                    lse_ref[hh, p, rows, :] = lse[u]
            else:
                per = ATTN_TILE // d4
                for u in range(nu):
                    rows = pl.ds(u // d4, blk, stride=d4)
                    stage_ref[0, u % d4, rows, :] = out[u]
                    stage_ref[1, u % d4, rows, :] = lse[u]
                for r4 in range(d4):
                    acc_ref[hh, p, pl.ds(r4, per, stride=d4), :] = stage_ref[0, r4]
                    lse_ref[hh, p, pl.ds(r4, per, stride=d4), :] = stage_ref[1, r4]

        lse = lse_ref[hh]
        w = jnp.exp(lse - jnp.max(lse, axis=0, keepdims=True))
        merged = jnp.sum(w * acc_ref[hh], axis=0) / jnp.sum(w, axis=0)
        o_ref[:, hh * HEAD_DIM:(hh + 1) * HEAD_DIM] = merged.astype(o_ref.dtype)


def _attention(qkv, *, batch, seq, heads_per_step=2):
    nt = seq // ATTN_TILE
    heads = N_KV_HEADS
    nu = ATTN_UNITS
    hps = heads_per_step
    assert heads % hps == 0
    q1, q4, q16, k1, k4, k16, v1, v4, v16 = qkv

    def cur():
        return pl.BlockSpec((nu, hps, ATTN_SPAN, HEAD_DIM), lambda b, h, n: (b * nt + n, h, 0, 0))

    def prev(d):
        per = nu // d
        return pl.BlockSpec((d, hps, ATTN_SPAN, HEAD_DIM),
                            lambda b, h, n: (jnp.maximum((b * nt + n) * per - 1, 0), h, 0, 0))

    unit_bytes = ATTN_SPAN * HEAD_DIM * 2
    tile_f32 = ATTN_TILE * HEAD_DIM * 4
    vmem = (hps * (2 * (9 * nu + 2 * sum(DILATIONS)) * unit_bytes + 2 * tile_f32 // 2 + 6 * tile_f32)
            + 12 * tile_f32 + (4 << 20))
    return pl.pallas_call(
        _attn_kernel,
        grid=(batch, heads // hps, nt),
        in_specs=[cur(), cur(), cur(), cur(), cur(), cur(), prev(1), prev(4), prev(16),
                  cur(), cur(), cur(), prev(1), prev(4), prev(16)],
        out_specs=pl.BlockSpec((ATTN_TILE, hps * HEAD_DIM), lambda b, h, n: (b * nt + n, h)),
        out_shape=jax.ShapeDtypeStruct((batch * seq, heads * HEAD_DIM), BF16),
        scratch_shapes=[pltpu.VMEM((hps, len(DILATIONS), ATTN_TILE, HEAD_DIM), F32)] * 2
        + [pltpu.VMEM((2, DILATIONS[1], ATTN_TILE // DILATIONS[1], HEAD_DIM), F32)],
        compiler_params=_compiler_params(("parallel", "parallel", "arbitrary"), vmem),
        name="dilated_attention",
    )(q1, q4, q16, k1, k4, k16, k1, k4, k16, v1, v4, v16, v1, v4, v16)


SCAN_PAD_ROWS = V7X_SUBLANES


def _linear_scan_rows(a, u, h0, sa_ref, su_ref):
    rows, width = a.shape
    sub = V7X_SUBLANES
    run = rows // sub
    pitch = run + SCAN_PAD_ROWS
    lanes = V7X_LANES
    out = []
    carry = []
    for c in range(width // lanes):
        cols = slice(c * lanes, (c + 1) * lanes)
        for s in range(sub):
            sa_ref[c, s * pitch:s * pitch + run, :] = a[s * run:(s + 1) * run, cols]
            su_ref[c, s * pitch:s * pitch + run, :] = u[s * run:(s + 1) * run, cols]
        h = jnp.zeros((sub, lanes), F32)
        prod = jnp.ones((sub, lanes), F32)
        hs, prods = [], []
        for k in range(run):
            ak = sa_ref[c, pl.ds(k, sub, stride=pitch), :]
            uk = su_ref[c, pl.ds(k, sub, stride=pitch), :]
            h = ak * h + uk
            prod = ak * prod
            hs.append(h)
            prods.append(prod)
        start = h0[:, cols]
        enter = [start]
        for s in range(sub - 1):
            start = h[s:s + 1, :] + prod[s:s + 1, :] * start
            enter.append(start)
        carry.append(h[sub - 1:sub, :] + prod[sub - 1:sub, :] * start)
        enter = jnp.concatenate(enter, axis=0)
        for k in range(run):
            su_ref[c, pl.ds(k, sub, stride=pitch), :] = hs[k] + prods[k] * enter
        out.append(jnp.concatenate([su_ref[c, s * pitch:s * pitch + run, :] for s in range(sub)], axis=0))
    return jnp.concatenate(out, axis=1), jnp.concatenate(carry, axis=1)


def _gelu_tanh(y):
    z = 0.7978845608028654 * (y + 0.044715 * (y * y * y))
    return y * jax.nn.sigmoid(2.0 * z)


def _lru_kernel(hin_ref, *refs, nc):
    wx_refs, wy_refs = refs[:nc], refs[nc:2 * nc]
    cw_ref, cb_ref, wri_ref, br_ref, bi_ref, lam_ref, o_ref, xs_ref, ys_ref, h_ref, sa_ref, su_ref = refs[2 * nc:]
    tl = hin_ref.shape[0]
    tc = wx_refs[0].shape[1]
    pad = V7X_SUBLANES
    s = pl.program_id(1)

    @pl.when(s == 0)
    def _():
        xs_ref[...] = jnp.zeros_like(xs_ref)
        ys_ref[...] = jnp.zeros_like(ys_ref)

    @pl.when(s <= 1)
    def _():
        xs_ref[:, 0:pad, :] = jnp.zeros((nc, pad, tc), F32)
        h_ref[...] = jnp.zeros_like(h_ref)

    half = tc // 2
    bw = LRU_BLOCK_WIDTH
    gpc = tc // bw
    for c in range(nc):
        ch = slice(c * tc, (c + 1) * tc)

        def project(dst_ref, rows, w_ref, part):
            cols = slice(part * half, (part + 1) * half)
            dst_ref[c, rows, cols] = jnp.dot(hin_ref[...], w_ref[:, cols], preferred_element_type=F32)

        x = xs_ref[c, pad:pad + tl, :]
        y = ys_ref[c]
        project(ys_ref, slice(0, tl), wy_refs[c], 0)
        xc = cw_ref[CONV_WIDTH - 1:CONV_WIDTH, ch] * x
        for j in range(CONV_WIDTH - 1):
            shift = CONV_WIDTH - 1 - j
            xc = xc + cw_ref[j:j + 1, ch] * xs_ref[c, pad - shift:pad - shift + tl, :]
        xc = xc + cb_ref[:, ch]
        xs_ref[c, 0:pad, :] = x[tl - pad:tl]

        pre = [jnp.dot(xc[:, g * bw:(g + 1) * bw].astype(BF16), wri_ref[c * gpc + g], preferred_element_type=F32)
               for g in range(gpc)]
        project(ys_ref, slice(0, tl), wy_refs[c], 1)
        project(xs_ref, slice(pad, pad + tl), wx_refs[c], 0)

        r = jax.nn.sigmoid(jnp.concatenate([t[:, :bw] for t in pre], axis=1) + br_ref[:, ch])
        i = jax.nn.sigmoid(jnp.concatenate([t[:, bw:] for t in pre], axis=1) + bi_ref[:, ch])
        neg_lam = -lam_ref[:, ch]
        softplus = jnp.maximum(neg_lam, 0.0) + jnp.log1p(jnp.exp(-jnp.abs(neg_lam)))
        log_a = (-LRU_C * softplus) * r
        a = jnp.exp(log_a)
        one_m_a2 = 1.0 - a * a
        root = jnp.where(one_m_a2 > 0.0, one_m_a2 * lax.rsqrt(one_m_a2), 0.0)
        u = root * (i * xc)
        project(xs_ref, slice(pad, pad + tl), wx_refs[c], 1)
        h, h_ref[c] = _linear_scan_rows(a, u, h_ref[c], sa_ref.at[c], su_ref.at[c])
        o_ref[:, ch] = (h * _gelu_tanh(y)).astype(o_ref.dtype)


def _rg_lru(hin, w_in, conv_w, conv_b, w_ri, b_r, b_i, lam, *, batch, seq, x_col, width, tl=256, tc=512):
    m, d = hin.shape
    nt = seq // tl
    nc = width // tc
    assert seq % tl == 0 and width % tc == 0 and x_col % tc == 0
    whole = lambda shape: pl.BlockSpec(shape, lambda b, n: (0,) * len(shape), pipeline_mode=pl.Buffered(1))
    w_block = lambda col: pl.BlockSpec((d, tc), lambda b, n: (0, col), pipeline_mode=pl.Buffered(1))
    vmem = (2 * tl * d * 2 + 2 * nc * d * tc * 2 + 2 * tl * width * 2 + width * 256 * 2
            + (4 * tl + 6 * V7X_SUBLANES * SCAN_PAD_ROWS) * width * 4 + 16 * tl * tc * 4 + (4 << 20))
    return pl.pallas_call(
        functools.partial(_lru_kernel, nc=nc),
        grid=(batch, nt + 1),
        in_specs=[pl.BlockSpec((tl, d), lambda b, n: (b * nt + jnp.minimum(n, nt - 1), 0))]
        + [w_block(x_col // tc + c) for c in range(2 * nc)]
        + [whole((CONV_WIDTH, width)), whole((1, width)),
           whole((width // LRU_BLOCK_WIDTH, LRU_BLOCK_WIDTH, 2 * LRU_BLOCK_WIDTH)),
           whole((1, width)), whole((1, width)), whole((1, width))],
        out_specs=pl.BlockSpec((tl, width), lambda b, n: (b * nt + jnp.maximum(n - 1, 0), 0)),
        out_shape=jax.ShapeDtypeStruct((m, width), BF16),
        scratch_shapes=[pltpu.VMEM((nc, tl + V7X_SUBLANES, tc), F32), pltpu.VMEM((nc, tl, tc), F32),
                        pltpu.VMEM((nc, 1, tc), F32)]
        + [pltpu.VMEM((nc, tc // V7X_LANES, tl + V7X_SUBLANES * SCAN_PAD_ROWS, V7X_LANES), F32)] * 2,
        compiler_params=_compiler_params(("parallel", "arbitrary"), vmem),
        name="rg_lru",
    )(hin, *([w_in] * (2 * nc)), conv_w, conv_b, w_ri, b_r, b_i, lam)


def _resident(shape):
    return pl.BlockSpec(shape, lambda i: (0,) * len(shape), pipeline_mode=pl.Buffered(1))


def _out_ln_kernel(x_ref, a_ref, r_ref, w_ref, g_ref, b_ref, o_ref, wb_ref, *, alpha, row_chunks):
    @pl.when(pl.program_id(0) == 0)
    def _():
        wb_ref[...] = w_ref[...].astype(BF16)

    rows = x_ref.shape[0] // row_chunks
    ka = a_ref.shape[1]
    for c in range(row_chunks):
        sl = slice(c * rows, (c + 1) * rows)
        mix = (jnp.dot(a_ref[sl, :], wb_ref[:ka, :], preferred_element_type=F32)
               + jnp.dot(r_ref[sl, :], wb_ref[ka:, :], preferred_element_type=F32))
        o_ref[sl, :] = _layer_norm_rows(alpha * x_ref[sl, :] + mix, g_ref[...], b_ref[...])


def _out_ln(x, attn, rec, w, g, b, *, alpha, tm=512, row_chunks=2):
    m, d = x.shape
    ka, kr = attn.shape[1], rec.shape[1]
    assert m % tm == 0 and tm % row_chunks == 0 and w.shape == (ka + kr, d)
    vmem = (2 * 2 * tm * d * 4 + 2 * tm * (ka + kr) * 2 + (ka + kr) * d * (4 + 2) + 3 * tm * d * 4 + (4 << 20))
    return pl.pallas_call(
        functools.partial(_out_ln_kernel, alpha=alpha, row_chunks=row_chunks),
        grid=(m // tm,),
        in_specs=[
            pl.BlockSpec((tm, d), lambda i: (i, 0)),
            pl.BlockSpec((tm, ka), lambda i: (i, 0)),
            pl.BlockSpec((tm, kr), lambda i: (i, 0)),
            _resident((ka + kr, d)), _resident((1, d)), _resident((1, d)),
        ],
        out_specs=pl.BlockSpec((tm, d), lambda i: (i, 0)),
        out_shape=jax.ShapeDtypeStruct((m, d), F32),
        scratch_shapes=[pltpu.VMEM((ka + kr, d), BF16)],
        compiler_params=_compiler_params(("arbitrary",), vmem),
        name="out_ln",
    )(x, attn, rec, w, g, b)


def _ple_kernel(x_ref, p_ref, wg_ref, wp_ref, o_ref, wgb_ref, wpb_ref):
    @pl.when(pl.program_id(0) == 0)
    def _():
        wgb_ref[...] = wg_ref[...].astype(BF16)
        wpb_ref[...] = wp_ref[...].astype(BF16)

    x = x_ref[...]
    gate = jax.nn.sigmoid(jnp.dot(x.astype(BF16), wgb_ref[...], preferred_element_type=F32))
    emb = jnp.dot(p_ref[...].astype(BF16), wpb_ref[...], preferred_element_type=F32)
    o_ref[...] = x + gate * emb


def _ple(x, p, wg, wp, *, tm=512):
    m, d = x.shape
    dp = p.shape[1]
    assert m % tm == 0
    vmem = 2 * 2 * tm * d * 4 + 2 * tm * dp * 4 + (d + dp) * d * (4 + 2) + 3 * tm * d * 4 + (4 << 20)
    return pl.pallas_call(
        _ple_kernel,
        grid=(m // tm,),
        in_specs=[
            pl.BlockSpec((tm, d), lambda i: (i, 0)),
            pl.BlockSpec((tm, dp), lambda i: (i, 0)),
            _resident((d, d)), _resident((dp, d)),
        ],
        out_specs=pl.BlockSpec((tm, d), lambda i: (i, 0)),
        out_shape=jax.ShapeDtypeStruct((m, d), F32),
        scratch_shapes=[pltpu.VMEM((d, d), BF16), pltpu.VMEM((dp, d), BF16)],
        compiler_params=_compiler_params(("arbitrary",), vmem),
        name="ple",
    )(x, p, wg, wp)


def kernel(x, p, positions, ffn1_w_gate, ffn1_w_up, ffn1_w_down, ln1_g, ln1_b, w_in, conv_w, conv_b, w_rgate, b_rgate, w_igate, b_igate, lru_lambda, w_out, ln2_g, ln2_b, ffn2_w_gate, ffn2_w_up, ffn2_w_down, ln3_g, ln3_b, w_ple_proj, w_ple_gate):
    batch, seq, d_model = x.shape
    depth = ffn1_w_gate.shape[0]
    m = batch * seq
    kv_width = N_KV_HEADS * HEAD_DIM
    qkv_width = (len(DILATIONS) + 2) * kv_width
    lru_width = d_model - kv_width
    alpha = (2.0 * depth) ** 0.25
    assert w_in.shape[2] == qkv_width + 2 * lru_width

    half = ROT_DIMS // 2
    inv_freq = jnp.power(jnp.float32(ROPE_THETA), -jnp.arange(half, dtype=F32) * (2.0 / ROT_DIMS))
    invf = jnp.zeros((1, V7X_LANES), F32).at[0, :ROT_DIMS].set(jnp.tile(inv_freq, 2))
    pos = positions.reshape(m, 1)
    row = lambda v: v.reshape(1, -1)

    h = x.reshape(m, d_model)
    for i in range(depth):
        h, (*ffn2_w, w_in_bf16) = _ffn_ln(
            h, ffn1_w_gate[i].astype(BF16), ffn1_w_up[i].astype(BF16), ffn1_w_down[i].astype(BF16),
            row(ln1_g[i]), row(ln1_b[i]), alpha=alpha, layer=i,
            cast_payload=(ffn2_w_gate, ffn2_w_up, ffn2_w_down, w_in))
        *qkv, hb = _qkv_proj(h, pos, invf, w_in_bf16, n=qkv_width)
        attn = _attention(qkv, batch=batch, seq=seq)
        w_ri = jnp.concatenate([w_rgate[i], w_igate[i]], axis=-1).astype(BF16)
        rec = _rg_lru(hb, w_in_bf16, conv_w[i], row(conv_b[i]), w_ri, row(b_rgate[i]), row(b_igate[i]),
                      row(lru_lambda[i]), batch=batch, seq=seq, x_col=qkv_width, width=lru_width)
        h = _out_ln(h, attn, rec, w_out[i], row(ln2_g[i]), row(ln2_b[i]), alpha=alpha)
        h, _ = _ffn_ln(h, *ffn2_w, row(ln3_g[i]), row(ln3_b[i]), alpha=alpha)
        h = _ple(h, p[i].reshape(m, -1), w_ple_gate[i], w_ple_proj[i])
    return h.reshape(batch, seq, d_model)
```

```python
import functools

import jax
import jax.numpy as jnp
from jax import lax
from jax.experimental import pallas as pl
from jax.experimental.pallas import tpu as pltpu

F32 = jnp.float32
BF16 = jnp.bfloat16

HEAD_DIM = 128
N_KV_HEADS = 4
DILATIONS = (1, 4, 16)
ATTN_SPAN = 128
ROT_DIMS = HEAD_DIM // 4
ROPE_THETA = 500000.0
LRU_BLOCK_WIDTH = 128
CONV_WIDTH = 4
LRU_C = 8.0
LN_EPS = 1e-5

V7X_LANES = 128
V7X_SUBLANES = 8
V7X_VMEM_SCOPED_BYTES = 60000 * 1024

ATTN_TILE = ATTN_SPAN * DILATIONS[-1]
ATTN_UNITS = ATTN_TILE // ATTN_SPAN
QKV_ROWS = ATTN_SPAN * DILATIONS[1]


def _compiler_params(semantics, vmem_estimate_bytes):
    limit = min(int(vmem_estimate_bytes), V7X_VMEM_SCOPED_BYTES)
    return pltpu.CompilerParams(dimension_semantics=semantics, vmem_limit_bytes=limit)


def _layer_norm_rows(y, g, b):
    mu = jnp.mean(y, axis=-1, keepdims=True)
    yc = y - mu
    var = jnp.mean(yc * yc, axis=-1, keepdims=True)
    return yc * lax.rsqrt(var + LN_EPS) * g + b


def _ffn_ln_kernel(x_ref, wg_ref, wu_ref, wd_ref, g_ref, b_ref, *rest, alpha, ln_row_chunks, n_cast):
    cast_in, o_ref, cast_out = rest[:n_cast], rest[n_cast], rest[n_cast + 1:]
    f = pl.program_id(1)
    last = pl.num_programs(1) - 1

    def cast_payload():
        for src, dst in zip(cast_in, cast_out):
            dst[...] = src[...].astype(BF16)

    def partial_sum(rows):
        xb = x_ref[rows, :].astype(BF16)
        gate = jnp.dot(xb, wg_ref[...], preferred_element_type=F32)
        up = jnp.dot(xb, wu_ref[...], preferred_element_type=F32)
        act = (jax.nn.silu(gate) * up).astype(BF16)
        return jnp.dot(act, wd_ref[...], preferred_element_type=F32)

    def accumulated(rows):
        return o_ref[rows, :] + partial_sum(rows)

    @pl.when(f == 0)
    def _():
        cast_payload()
        o_ref[...] = partial_sum(slice(None))

    @pl.when(jnp.logical_and(f > 0, f < last))
    def _():
        cast_payload()
        o_ref[...] = accumulated(slice(None))

    @pl.when(f == last)
    def _():
        cast_payload()
        rows_per = x_ref.shape[0] // ln_row_chunks
        for c in range(ln_row_chunks):
            rows = slice(c * rows_per, (c + 1) * rows_per)
            y = alpha * x_ref[rows, :] + 0.5 * accumulated(rows)
            o_ref[rows, :] = _layer_norm_rows(y, g_ref[...], b_ref[...])


def _ffn_ln(x, wg, wu, wd, g, b, *, alpha, cast_payload=(), layer=0, tm=1024, tf=512, ln_row_chunks=4):
    m, d = x.shape
    dff = wg.shape[1]
    assert m % tm == 0 and dff % tf == 0
    n_tiles, nf = m // tm, dff // tf
    pay_in, pay_out = [], []
    for a in cast_payload:
        _, rows, cols = a.shape
        if rows % (n_tiles * nf * 2 * V7X_SUBLANES) == 0:
            blk, idx = (rows // (n_tiles * nf), cols), (lambda i, f: (i * nf + f, 0))
        elif rows % n_tiles == 0 and cols % nf == 0:
            blk, idx = (rows // n_tiles, cols // nf), (lambda i, f: (i, f))
        else:
            blk, idx = (rows // nf, cols // n_tiles), (lambda i, f: (f, i))
        assert blk[0] * blk[1] * n_tiles * nf == rows * cols
        assert blk[0] % (2 * V7X_SUBLANES) == 0 and blk[1] % V7X_LANES == 0, blk
        pay_in.append(pl.BlockSpec((None,) + blk, lambda i, f, idx=idx: (layer,) + idx(i, f)))
        pay_out.append(pl.BlockSpec(blk, idx))
    vmem = (2 * tm * d * (4 + 4) + tm * d * 2 + 2 * 3 * d * tf * 2
            + 3 * tm * tf * 4 + (6 << 20)
            + sum(2 * s.block_shape[0] * s.block_shape[1] * (4 + 2) for s in pay_out))
    out, *cast = pl.pallas_call(
        functools.partial(_ffn_ln_kernel, alpha=alpha, ln_row_chunks=ln_row_chunks, n_cast=len(cast_payload)),
        grid=(n_tiles, nf),
        in_specs=[
            pl.BlockSpec((tm, d), lambda i, f: (i, 0)),
            pl.BlockSpec((d, tf), lambda i, f: (0, f)),
            pl.BlockSpec((d, tf), lambda i, f: (0, f)),
            pl.BlockSpec((tf, d), lambda i, f: (f, 0)),
            pl.BlockSpec((1, d), lambda i, f: (0, 0)),
            pl.BlockSpec((1, d), lambda i, f: (0, 0)),
        ] + pay_in,
        out_specs=[pl.BlockSpec((tm, d), lambda i, f: (i, 0))] + pay_out,
        out_shape=[jax.ShapeDtypeStruct((m, d), F32)]
        + [jax.ShapeDtypeStruct(a.shape[1:], BF16) for a in cast_payload],
        compiler_params=_compiler_params(("parallel", "arbitrary"), vmem),
        name="ffn_ln",
    )(x, wg, wu, wd, g, b, *cast_payload)
    return out, cast


def _qkv_proj_kernel(pos_ref, invf_ref, x_ref, w_ref,
                     q1_ref, q4_ref, q16_ref, k1_ref, k4_ref, k16_ref, v1_ref, v4_ref, v16_ref, xb_ref,
                     slab_ref):
    tm = x_ref.shape[0]
    heads = N_KV_HEADS
    half = ROT_DIMS // 2
    kvw = heads * HEAD_DIM
    x = x_ref[...].astype(BF16)
    xb_ref[...] = x

    ang = pos_ref[...].astype(F32) * invf_ref[...]
    lane = lax.broadcasted_iota(jnp.int32, (tm, V7X_LANES), 1)
    per_row = V7X_LANES // half

    def spread(packed, fill):
        rows = jnp.repeat(packed, per_row, axis=0)
        lo = pltpu.roll(rows, 0, 1, stride=half, stride_axis=0)
        return jnp.where(lane < half, lo, jnp.where(lane < ROT_DIMS, pltpu.roll(lo, half, 1), fill))

    cos = spread(jnp.cos(ang), 1.0)
    sin = spread(jnp.sin(ang), 0.0)
    sin_lo = jnp.where(lane < half, -sin, 0.0)
    sin_hi = jnp.where(lane >= half, sin, 0.0)

    def rotate(t):
        return t * cos + pltpu.roll(t, HEAD_DIM - half, 1) * sin_lo + pltpu.roll(t, half, 1) * sin_hi

    def write_units(t, h, slab, o1_ref, o4_ref, o16_ref):
        if o1_ref is not None:
            for jb in range(tm // ATTN_SPAN):
                o1_ref[jb, h] = t[jb * ATTN_SPAN:(jb + 1) * ATTN_SPAN].astype(BF16)
        if o4_ref is None and o16_ref is None:
            return
        d4 = DILATIONS[1]
        slab_ref[0, slab] = t
        for r4 in range(d4):
            rows4 = slab_ref[0, slab, pl.ds(r4, tm // d4, stride=d4), :]
            if o4_ref is not None:
                o4_ref[r4, h] = rows4.astype(BF16)
            if o16_ref is not None:
                slab_ref[1, slab, r4 * (tm // d4):(r4 + 1) * (tm // d4), :] = rows4
        if o16_ref is not None:
            d16 = DILATIONS[2]
            for r16 in range(d16):
                r4, c = r16 % d4, r16 // d4
                start = r4 * (tm // d4) + c
                o16_ref[r16, h] = slab_ref[1, slab, pl.ds(start, tm // d16, stride=d16 // d4), :].astype(BF16)

    groups = (
        (0, True, (q1_ref, None, None)), (1, True, (None, q4_ref, None)), (2, True, (None, None, q16_ref)),
        (3, True, (k1_ref, k4_ref, k16_ref)), (4, False, (v1_ref, v4_ref, v16_ref)),
    )
    for step, (j, rot, outs) in enumerate(groups):
        acc = jnp.dot(x, w_ref[:, j * kvw:(j + 1) * kvw], preferred_element_type=F32)
        for h in range(heads):
            t = acc[:, h * HEAD_DIM:(h + 1) * HEAD_DIM]
            write_units(rotate(t) if rot else t, h, (step * heads + h) % slab_ref.shape[1], *outs)


def _qkv_proj(x, pos, invf, w, *, n):
    m, d = x.shape
    tm = QKV_ROWS
    heads = N_KV_HEADS
    per_tile = ATTN_TILE // tm
    assert m % ATTN_TILE == 0 and n == 5 * heads * HEAD_DIM and w.shape[1] >= n
    units = lambda rows: jax.ShapeDtypeStruct((rows // ATTN_SPAN, heads, ATTN_SPAN, HEAD_DIM), BF16)
    flat_spec = pl.BlockSpec((tm // ATTN_SPAN, heads, ATTN_SPAN, HEAD_DIM), lambda i: (i, 0, 0, 0))
    wide_spec = pl.BlockSpec((DILATIONS[2], heads, tm // DILATIONS[2], HEAD_DIM),
                             lambda i: (i // per_tile, 0, i % per_tile, 0))
    out_specs = [flat_spec, flat_spec, wide_spec] * 3 + [pl.BlockSpec((tm, d), lambda i: (i, 0))]
    n_slabs = 4
    vmem = (2 * tm * d * 4 + tm * d * 2 + 2 * d * n * 2 + 2 * 9 * tm * heads * HEAD_DIM * 2 + 2 * tm * d * 2
            + 2 * n_slabs * tm * HEAD_DIM * 4 + 8 * tm * heads * HEAD_DIM * 4 + (6 << 20))
    return pl.pallas_call(
        _qkv_proj_kernel,
        grid=(m // tm,),
        in_specs=[
            pl.BlockSpec((tm * (ROT_DIMS // 2) // V7X_LANES, V7X_LANES), lambda i: (i, 0)),
            pl.BlockSpec((1, V7X_LANES), lambda i: (0, 0)),
            pl.BlockSpec((tm, d), lambda i: (i, 0)),
            pl.BlockSpec((d, n), lambda i: (0, 0)),
        ],
        out_specs=out_specs,
        out_shape=[units(m)] * 9 + [jax.ShapeDtypeStruct((m, d), BF16)],
        scratch_shapes=[pltpu.VMEM((2, n_slabs, tm, HEAD_DIM), F32)],
        compiler_params=_compiler_params(("parallel",), vmem),
        name="qkv_proj",
    )(pos, invf, x, w)


def _attn_kernel(q1_ref, q4_ref, q16_ref, k1c_ref, k4c_ref, k16c_ref, k1p_ref, k4p_ref, k16p_ref,
                 v1c_ref, v4c_ref, v16c_ref, v1p_ref, v4p_ref, v16p_ref, o_ref, acc_ref, lse_ref, stage_ref):
    first_tile = pl.program_id(2) == 0
    blk = ATTN_SPAN
    nu = ATTN_UNITS
    scale = HEAD_DIM ** -0.5
    qi = lax.broadcasted_iota(jnp.int32, (1, blk, blk), 1)
    ki = lax.broadcasted_iota(jnp.int32, (1, blk, blk), 2)
    unit = lax.broadcasted_iota(jnp.int32, (nu, 1, 1), 0)
    neg_inf = -jnp.inf
    qk = lambda a, b: jnp.einsum("uqd,ukd->uqk", a, b, preferred_element_type=F32)
    pv = lambda a, b: jnp.einsum("uqk,ukd->uqd", a, b, preferred_element_type=F32)

    def with_prev(fn, lhs, prev_ref, cur_ref, d, hh):
        head = fn(lhs[:d], prev_ref[:, hh])
        if d == nu:
            return head
        return jnp.concatenate([head, fn(lhs[d:], cur_ref[0:nu - d, hh])], axis=0)

    refs = ((q1_ref, k1c_ref, k1p_ref, v1c_ref, v1p_ref), (q4_ref, k4c_ref, k4p_ref, v4c_ref, v4p_ref),
            (q16_ref, k16c_ref, k16p_ref, v16c_ref, v16p_ref))
    for hh in range(q1_ref.shape[1]):
        for p, (d, (q_ref, kc_ref, kp_ref, vc_ref, vp_ref)) in enumerate(zip(DILATIONS, refs)):
            q = q_ref[:, hh]
            sp = with_prev(qk, q, kp_ref, kc_ref, d, hh) * scale
            sc = qk(q, kc_ref[:, hh]) * scale
            sp = jnp.where(ki >= qi, sp, neg_inf)
            sp = jnp.where(jnp.logical_and(first_tile, unit < d), neg_inf, sp)
            sc = jnp.where(ki <= qi, sc, neg_inf)
            mx = jnp.max(jnp.maximum(sp, sc), axis=-1, keepdims=True)
            ep = jnp.exp(sp - mx)
            ec = jnp.exp(sc - mx)
            den = jnp.sum(ep + ec, axis=-1, keepdims=True)
            out = (with_prev(pv, ep.astype(BF16), vp_ref, vc_ref, d, hh) + pv(ec.astype(BF16), vc_ref[:, hh])) / den
            lse = jnp.broadcast_to(mx + jnp.log(den), out.shape)
            d4 = DILATIONS[1]
            if d == 1:
                acc_ref[hh, p] = out.reshape(ATTN_TILE, HEAD_DIM)
                lse_ref[hh, p] = lse.reshape(ATTN_TILE, HEAD_DIM)
            elif d == d4:
                for u in range(nu):
                    jb, r = divmod(u, d)
                    rows = pl.ds(d * blk * jb + r, blk, stride=d)
                    acc_ref[hh, p, rows, :] = out[u]
                    lse_ref[hh, p, rows, :] = lse[u]
            else:
                per = ATTN_TILE // d4
                for u in range(nu):
                    rows = pl.ds(u // d4, blk, stride=d4)
                    stage_ref[0, u % d4, rows, :] = out[u]
                    stage_ref[1, u % d4, rows, :] = lse[u]
                for r4 in range(d4):
                    acc_ref[hh, p, pl.ds(r4, per, stride=d4), :] = stage_ref[0, r4]
                    lse_ref[hh, p, pl.ds(r4, per, stride=d4), :] = stage_ref[1, r4]

        lse = lse_ref[hh]
        w = jnp.exp(lse - jnp.max(lse, axis=0, keepdims=True))
        merged = jnp.sum(w * acc_ref[hh], axis=0) / jnp.sum(w, axis=0)
        o_ref[:, hh * HEAD_DIM:(hh + 1) * HEAD_DIM] = merged.astype(o_ref.dtype)


def _attention(qkv, *, batch, seq, heads_per_step=2):
    nt = seq // ATTN_TILE
    heads = N_KV_HEADS
    nu = ATTN_UNITS
    hps = heads_per_step
    assert heads % hps == 0
    q1, q4, q16, k1, k4, k16, v1, v4, v16 = qkv

    def cur():
        return pl.BlockSpec((nu, hps, ATTN_SPAN, HEAD_DIM), lambda b, h, n: (b * nt + n, h, 0, 0))

    def prev(d):
        per = nu // d
        return pl.BlockSpec((d, hps, ATTN_SPAN, HEAD_DIM),
                            lambda b, h, n: (jnp.maximum((b * nt + n) * per - 1, 0), h, 0, 0))

    unit_bytes = ATTN_SPAN * HEAD_DIM * 2
    tile_f32 = ATTN_TILE * HEAD_DIM * 4
    vmem = (hps * (2 * (9 * nu + 2 * sum(DILATIONS)) * unit_bytes + 2 * tile_f32 // 2 + 6 * tile_f32)
            + 12 * tile_f32 + (4 << 20))
    return pl.pallas_call(
        _attn_kernel,
        grid=(batch, heads // hps, nt),
        in_specs=[cur(), cur(), cur(), cur(), cur(), cur(), prev(1), prev(4), prev(16),
                  cur(), cur(), cur(), prev(1), prev(4), prev(16)],
        out_specs=pl.BlockSpec((ATTN_TILE, hps * HEAD_DIM), lambda b, h, n: (b * nt + n, h)),
        out_shape=jax.ShapeDtypeStruct((batch * seq, heads * HEAD_DIM), BF16),
        scratch_shapes=[pltpu.VMEM((hps, len(DILATIONS), ATTN_TILE, HEAD_DIM), F32)] * 2
        + [pltpu.VMEM((2, DILATIONS[1], ATTN_TILE // DILATIONS[1], HEAD_DIM), F32)],
        compiler_params=_compiler_params(("parallel", "parallel", "arbitrary"), vmem),
        name="dilated_attention",
    )(q1, q4, q16, k1, k4, k16, k1, k4, k16, v1, v4, v16, v1, v4, v16)


SCAN_PAD_ROWS = V7X_SUBLANES


def _linear_scan_rows(a, u, h0, sa_ref, su_ref):
    rows, width = a.shape
    sub = V7X_SUBLANES
    run = rows // sub
    pitch = run + SCAN_PAD_ROWS
    lanes = V7X_LANES
    out = []
    carry = []
    for c in range(width // lanes):
        cols = slice(c * lanes, (c + 1) * lanes)
        for s in range(sub):
            sa_ref[c, s * pitch:s * pitch + run, :] = a[s * run:(s + 1) * run, cols]
            su_ref[c, s * pitch:s * pitch + run, :] = u[s * run:(s + 1) * run, cols]
        h = jnp.zeros((sub, lanes), F32)
        prod = jnp.ones((sub, lanes), F32)
        hs, prods = [], []
        for k in range(run):
            ak = sa_ref[c, pl.ds(k, sub, stride=pitch), :]
            uk = su_ref[c, pl.ds(k, sub, stride=pitch), :]
            h = ak * h + uk
            prod = ak * prod
            hs.append(h)
            prods.append(prod)
        start = h0[:, cols]
        enter = [start]
        for s in range(sub - 1):
            start = h[s:s + 1, :] + prod[s:s + 1, :] * start
            enter.append(start)
        carry.append(h[sub - 1:sub, :] + prod[sub - 1:sub, :] * start)
        enter = jnp.concatenate(enter, axis=0)
        for k in range(run):
            su_ref[c, pl.ds(k, sub, stride=pitch), :] = hs[k] + prods[k] * enter
        out.append(jnp.concatenate([su_ref[c, s * pitch:s * pitch + run, :] for s in range(sub)], axis=0))
    return jnp.concatenate(out, axis=1), jnp.concatenate(carry, axis=1)


def _gelu_tanh(y):
    z = 0.7978845608028654 * (y + 0.044715 * (y * y * y))
    return y * jax.nn.sigmoid(2.0 * z)


def _lru_kernel(hin_ref, *refs, nc):
    wx_refs, wy_refs = refs[:nc], refs[nc:2 * nc]
    cw_ref, cb_ref, wri_ref, br_ref, bi_ref, lam_ref, o_ref, xs_ref, ys_ref, h_ref, sa_ref, su_ref = refs[2 * nc:]
    tl = hin_ref.shape[0]
    tc = wx_refs[0].shape[1]
    pad = V7X_SUBLANES
    s = pl.program_id(1)

    @pl.when(s == 0)
    def _():
        xs_ref[...] = jnp.zeros_like(xs_ref)
        ys_ref[...] = jnp.zeros_like(ys_ref)

    @pl.when(s <= 1)
    def _():
        xs_ref[:, 0:pad, :] = jnp.zeros((nc, pad, tc), F32)
        h_ref[...] = jnp.zeros_like(h_ref)

    half = tc // 2
    bw = LRU_BLOCK_WIDTH
    gpc = tc // bw
    for c in range(nc):
        ch = slice(c * tc, (c + 1) * tc)

        def project(dst_ref, rows, w_ref, part):
            cols = slice(part * half, (part + 1) * half)
            dst_ref[c, rows, cols] = jnp.dot(hin_ref[...], w_ref[:, cols], preferred_element_type=F32)

        x = xs_ref[c, pad:pad + tl, :]
        y = ys_ref[c]
        project(ys_ref, slice(0, tl), wy_refs[c], 0)
        xc = cw_ref[CONV_WIDTH - 1:CONV_WIDTH, ch] * x
        for j in range(CONV_WIDTH - 1):
            shift = CONV_WIDTH - 1 - j
            xc = xc + cw_ref[j:j + 1, ch] * xs_ref[c, pad - shift:pad - shift + tl, :]
        xc = xc + cb_ref[:, ch]
        xs_ref[c, 0:pad, :] = x[tl - pad:tl]

        pre = [jnp.dot(xc[:, g * bw:(g + 1) * bw].astype(BF16), wri_ref[c * gpc + g], preferred_element_type=F32)
               for g in range(gpc)]
        project(ys_ref, slice(0, tl), wy_refs[c], 1)
        project(xs_ref, slice(pad, pad + tl), wx_refs[c], 0)

        r = jax.nn.sigmoid(jnp.concatenate([t[:, :bw] for t in pre], axis=1) + br_ref[:, ch])
        i = jax.nn.sigmoid(jnp.concatenate([t[:, bw:] for t in pre], axis=1) + bi_ref[:, ch])
        neg_lam = -lam_ref[:, ch]
        softplus = jnp.maximum(neg_lam, 0.0) + jnp.log1p(jnp.exp(-jnp.abs(neg_lam)))
        log_a = (-LRU_C * softplus) * r
        a = jnp.exp(log_a)
        one_m_a2 = 1.0 - a * a
        root = jnp.where(one_m_a2 > 0.0, one_m_a2 * lax.rsqrt(one_m_a2), 0.0)
        u = root * (i * xc)
        project(xs_ref, slice(pad, pad + tl), wx_refs[c], 1)
        h, h_ref[c] = _linear_scan_rows(a, u, h_ref[c], sa_ref.at[c], su_ref.at[c])
        o_ref[:, ch] = (h * _gelu_tanh(y)).astype(o_ref.dtype)


def _rg_lru(hin, w_in, conv_w, conv_b, w_ri, b_r, b_i, lam, *, batch, seq, x_col, width, tl=256, tc=512):
    m, d = hin.shape
    nt = seq // tl
    nc = width // tc
    assert seq % tl == 0 and width % tc == 0 and x_col % tc == 0
    whole = lambda shape: pl.BlockSpec(shape, lambda b, n: (0,) * len(shape), pipeline_mode=pl.Buffered(1))
    w_block = lambda col: pl.BlockSpec((d, tc), lambda b, n: (0, col), pipeline_mode=pl.Buffered(1))
    vmem = (2 * tl * d * 2 + 2 * nc * d * tc * 2 + 2 * tl * width * 2 + width * 256 * 2
            + (4 * tl + 6 * V7X_SUBLANES * SCAN_PAD_ROWS) * width * 4 + 16 * tl * tc * 4 + (4 << 20))
    return pl.pallas_call(
        functools.partial(_lru_kernel, nc=nc),
        grid=(batch, nt + 1),
        in_specs=[pl.BlockSpec((tl, d), lambda b, n: (b * nt + jnp.minimum(n, nt - 1), 0))]
        + [w_block(x_col // tc + c) for c in range(2 * nc)]
        + [whole((CONV_WIDTH, width)), whole((1, width)),
           whole((width // LRU_BLOCK_WIDTH, LRU_BLOCK_WIDTH, 2 * LRU_BLOCK_WIDTH)),
           whole((1, width)), whole((1, width)), whole((1, width))],
        out_specs=pl.BlockSpec((tl, width), lambda b, n: (b * nt + jnp.maximum(n - 1, 0), 0)),
        out_shape=jax.ShapeDtypeStruct((m, width), BF16),
        scratch_shapes=[pltpu.VMEM((nc, tl + V7X_SUBLANES, tc), F32), pltpu.VMEM((nc, tl, tc), F32),
                        pltpu.VMEM((nc, 1, tc), F32)]
        + [pltpu.VMEM((nc, tc // V7X_LANES, tl + V7X_SUBLANES * SCAN_PAD_ROWS, V7X_LANES), F32)] * 2,
        compiler_params=_compiler_params(("parallel", "arbitrary"), vmem),
        name="rg_lru",
    )(hin, *([w_in] * (2 * nc)), conv_w, conv_b, w_ri, b_r, b_i, lam)


def _resident(shape):
    return pl.BlockSpec(shape, lambda i: (0,) * len(shape), pipeline_mode=pl.Buffered(1))


def _out_ln_kernel(x_ref, a_ref, r_ref, w_ref, g_ref, b_ref, o_ref, wb_ref, *, alpha, row_chunks):
    @pl.when(pl.program_id(0) == 0)
    def _():
        wb_ref[...] = w_ref[...].astype(BF16)

    rows = x_ref.shape[0] // row_chunks
    ka = a_ref.shape[1]
    for c in range(row_chunks):
        sl = slice(c * rows, (c + 1) * rows)
        mix = (jnp.dot(a_ref[sl, :], wb_ref[:ka, :], preferred_element_type=F32)
               + jnp.dot(r_ref[sl, :], wb_ref[ka:, :], preferred_element_type=F32))
        o_ref[sl, :] = _layer_norm_rows(alpha * x_ref[sl, :] + mix, g_ref[...], b_ref[...])


def _out_ln(x, attn, rec, w, g, b, *, alpha, tm=512, row_chunks=2):
    m, d = x.shape
    ka, kr = attn.shape[1], rec.shape[1]
    assert m % tm == 0 and tm % row_chunks == 0 and w.shape == (ka + kr, d)
    vmem = (2 * 2 * tm * d * 4 + 2 * tm * (ka + kr) * 2 + (ka + kr) * d * (4 + 2) + 3 * tm * d * 4 + (4 << 20))
    return pl.pallas_call(
        functools.partial(_out_ln_kernel, alpha=alpha, row_chunks=row_chunks),
        grid=(m // tm,),
        in_specs=[
            pl.BlockSpec((tm, d), lambda i: (i, 0)),
            pl.BlockSpec((tm, ka), lambda i: (i, 0)),
            pl.BlockSpec((tm, kr), lambda i: (i, 0)),
            _resident((ka + kr, d)), _resident((1, d)), _resident((1, d)),
        ],
        out_specs=pl.BlockSpec((tm, d), lambda i: (i, 0)),
        out_shape=jax.ShapeDtypeStruct((m, d), F32),
        scratch_shapes=[pltpu.VMEM((ka + kr, d), BF16)],
        compiler_params=_compiler_params(("arbitrary",), vmem),
        name="out_ln",
    )(x, attn, rec, w, g, b)


def _ple_kernel(x_ref, p_ref, wg_ref, wp_ref, o_ref, wgb_ref, wpb_ref):
    @pl.when(pl.program_id(0) == 0)
    def _():
        wgb_ref[...] = wg_ref[...].astype(BF16)
        wpb_ref[...] = wp_ref[...].astype(BF16)

    x = x_ref[...]
    gate = jax.nn.sigmoid(jnp.dot(x.astype(BF16), wgb_ref[...], preferred_element_type=F32))
    emb = jnp.dot(p_ref[...].astype(BF16), wpb_ref[...], preferred_element_type=F32)
    o_ref[...] = x + gate * emb


def _ple(x, p, wg, wp, *, tm=512):
    m, d = x.shape
    dp = p.shape[1]
    assert m % tm == 0
    vmem = 2 * 2 * tm * d * 4 + 2 * tm * dp * 4 + (d + dp) * d * (4 + 2) + 3 * tm * d * 4 + (4 << 20)
    return pl.pallas_call(
        _ple_kernel,
        grid=(m // tm,),
        in_specs=[
            pl.BlockSpec((tm, d), lambda i: (i, 0)),
            pl.BlockSpec((tm, dp), lambda i: (i, 0)),
            _resident((d, d)), _resident((dp, d)),
        ],
        out_specs=pl.BlockSpec((tm, d), lambda i: (i, 0)),
        out_shape=jax.ShapeDtypeStruct((m, d), F32),
        scratch_shapes=[pltpu.VMEM((d, d), BF16), pltpu.VMEM((dp, d), BF16)],
        compiler_params=_compiler_params(("arbitrary",), vmem),
        name="ple",
    )(x, p, wg, wp)


def kernel(x, p, positions, ffn1_w_gate, ffn1_w_up, ffn1_w_down, ln1_g, ln1_b, w_in, conv_w, conv_b, w_rgate, b_rgate, w_igate, b_igate, lru_lambda, w_out, ln2_g, ln2_b, ffn2_w_gate, ffn2_w_up, ffn2_w_down, ln3_g, ln3_b, w_ple_proj, w_ple_gate):
    batch, seq, d_model = x.shape
    depth = ffn1_w_gate.shape[0]
    m = batch * seq
    kv_width = N_KV_HEADS * HEAD_DIM
    qkv_width = (len(DILATIONS) + 2) * kv_width
    lru_width = d_model - kv_width
    alpha = (2.0 * depth) ** 0.25
    assert w_in.shape[2] == qkv_width + 2 * lru_width

    half = ROT_DIMS // 2
    inv_freq = jnp.power(jnp.float32(ROPE_THETA), -jnp.arange(half, dtype=F32) * (2.0 / ROT_DIMS))
    per_row = V7X_LANES // half
    invf = jnp.tile(inv_freq, per_row).reshape(1, V7X_LANES)
    order = jnp.array([(per_row - b) % per_row for b in range(per_row)])
    pos = jnp.repeat(positions.reshape(m // per_row, per_row)[:, order], half, axis=1)
    row = lambda v: v.reshape(1, -1)

    h = x.reshape(m, d_model)
    for i in range(depth):
        h, (*ffn2_w, w_in_bf16) = _ffn_ln(
            h, ffn1_w_gate[i].astype(BF16), ffn1_w_up[i].astype(BF16), ffn1_w_down[i].astype(BF16),
            row(ln1_g[i]), row(ln1_b[i]), alpha=alpha, layer=i,
            cast_payload=(ffn2_w_gate, ffn2_w_up, ffn2_w_down, w_in))
        *qkv, hb = _qkv_proj(h, pos, invf, w_in_bf16, n=qkv_width)
        attn = _attention(qkv, batch=batch, seq=seq)
        w_ri = jnp.concatenate([w_rgate[i], w_igate[i]], axis=-1).astype(BF16)
        rec = _rg_lru(hb, w_in_bf16, conv_w[i], row(conv_b[i]), w_ri, row(b_rgate[i]), row(b_igate[i]),
                      row(lru_lambda[i]), batch=batch, seq=seq, x_col=qkv_width, width=lru_width)
        h = _out_ln(h, attn, rec, w_out[i], row(ln2_g[i]), row(ln2_b[i]), alpha=alpha)
        h, _ = _ffn_ln(h, *ffn2_w, row(ln3_g[i]), row(ln3_b[i]), alpha=alpha)
        h = _ple(h, p[i].reshape(m, -1), w_ple_gate[i], w_ple_proj[i])
    return h.reshape(batch, seq, d_model)
```

```python
import functools

import jax
import jax.numpy as jnp
from jax import lax
from jax.experimental import pallas as pl
from jax.experimental.pallas import tpu as pltpu

F32 = jnp.float32
BF16 = jnp.bfloat16

HEAD_DIM = 128
N_KV_HEADS = 4
DILATIONS = (1, 4, 16)
ATTN_SPAN = 128
ROT_DIMS = HEAD_DIM // 4
ROPE_THETA = 500000.0
LRU_BLOCK_WIDTH = 128
CONV_WIDTH = 4
LRU_C = 8.0
LN_EPS = 1e-5

V7X_LANES = 128
V7X_SUBLANES = 8
V7X_VMEM_SCOPED_BYTES = 60000 * 1024

ATTN_TILE = ATTN_SPAN * DILATIONS[-1]
ATTN_UNITS = ATTN_TILE // ATTN_SPAN
QKV_ROWS = ATTN_SPAN * DILATIONS[1]


def _compiler_params(semantics, vmem_estimate_bytes):
    limit = min(int(vmem_estimate_bytes), V7X_VMEM_SCOPED_BYTES)
    return pltpu.CompilerParams(dimension_semantics=semantics, vmem_limit_bytes=limit)


def _layer_norm_rows(y, g, b):
    mu = jnp.mean(y, axis=-1, keepdims=True)
    yc = y - mu
    var = jnp.mean(yc * yc, axis=-1, keepdims=True)
    return yc * lax.rsqrt(var + LN_EPS) * g + b


def _ffn_ln_kernel(x_ref, wg_ref, wu_ref, wd_ref, g_ref, b_ref, *rest, alpha, ln_row_chunks, n_cast):
    cast_in, o_ref, cast_out = rest[:n_cast], rest[n_cast], rest[n_cast + 1:]
    f = pl.program_id(1)
    last = pl.num_programs(1) - 1

    def cast_payload():
        for src, dst in zip(cast_in, cast_out):
            dst[...] = src[...].astype(BF16)

    def partial_sum(rows):
        xb = x_ref[rows, :].astype(BF16)
        gate = jnp.dot(xb, wg_ref[...], preferred_element_type=F32)
        up = jnp.dot(xb, wu_ref[...], preferred_element_type=F32)
        act = (jax.nn.silu(gate) * up).astype(BF16)
        return jnp.dot(act, wd_ref[...], preferred_element_type=F32)

    def accumulated(rows):
        return o_ref[rows, :] + partial_sum(rows)

    @pl.when(f == 0)
    def _():
        cast_payload()
        o_ref[...] = partial_sum(slice(None))

    @pl.when(jnp.logical_and(f > 0, f < last))
    def _():
        cast_payload()
        o_ref[...] = accumulated(slice(None))

    @pl.when(f == last)
    def _():
        cast_payload()
        rows_per = x_ref.shape[0] // ln_row_chunks
        for c in range(ln_row_chunks):
            rows = slice(c * rows_per, (c + 1) * rows_per)
            y = alpha * x_ref[rows, :] + 0.5 * accumulated(rows)
            o_ref[rows, :] = _layer_norm_rows(y, g_ref[...], b_ref[...])


def _ffn_ln(x, wg, wu, wd, g, b, *, alpha, cast_payload=(), layer=0, tm=1024, tf=512, ln_row_chunks=4):
    m, d = x.shape
    dff = wg.shape[1]
    assert m % tm == 0 and dff % tf == 0
    n_tiles, nf = m // tm, dff // tf
    pay_in, pay_out = [], []
    for a in cast_payload:
        _, rows, cols = a.shape
        if rows % (n_tiles * nf * 2 * V7X_SUBLANES) == 0:
            blk, idx = (rows // (n_tiles * nf), cols), (lambda i, f: (i * nf + f, 0))
        elif rows % n_tiles == 0 and cols % nf == 0:
            blk, idx = (rows // n_tiles, cols // nf), (lambda i, f: (i, f))
        else:
            blk, idx = (rows // nf, cols // n_tiles), (lambda i, f: (f, i))
        assert blk[0] * blk[1] * n_tiles * nf == rows * cols
        assert blk[0] % (2 * V7X_SUBLANES) == 0 and blk[1] % V7X_LANES == 0, blk
        pay_in.append(pl.BlockSpec((None,) + blk, lambda i, f, idx=idx: (layer,) + idx(i, f)))
        pay_out.append(pl.BlockSpec(blk, idx))
    vmem = (2 * tm * d * (4 + 4) + tm * d * 2 + 2 * 3 * d * tf * 2
            + 3 * tm * tf * 4 + (6 << 20)
            + sum(2 * s.block_shape[0] * s.block_shape[1] * (4 + 2) for s in pay_out))
    out, *cast = pl.pallas_call(
        functools.partial(_ffn_ln_kernel, alpha=alpha, ln_row_chunks=ln_row_chunks, n_cast=len(cast_payload)),
        grid=(n_tiles, nf),
        in_specs=[
            pl.BlockSpec((tm, d), lambda i, f: (i, 0)),
            pl.BlockSpec((d, tf), lambda i, f: (0, f)),
            pl.BlockSpec((d, tf), lambda i, f: (0, f)),
            pl.BlockSpec((tf, d), lambda i, f: (f, 0)),
            pl.BlockSpec((1, d), lambda i, f: (0, 0)),
            pl.BlockSpec((1, d), lambda i, f: (0, 0)),
        ] + pay_in,
        out_specs=[pl.BlockSpec((tm, d), lambda i, f: (i, 0))] + pay_out,
        out_shape=[jax.ShapeDtypeStruct((m, d), F32)]
        + [jax.ShapeDtypeStruct(a.shape[1:], BF16) for a in cast_payload],
        compiler_params=_compiler_params(("parallel", "arbitrary"), vmem),
        name="ffn_ln",
    )(x, wg, wu, wd, g, b, *cast_payload)
    return out, cast


def _qkv_proj_kernel(pos_ref, invf_ref, x_ref, w_ref,
                     q1_ref, q4_ref, q16_ref, k1_ref, k4_ref, k16_ref, v1_ref, v4_ref, v16_ref, xb_ref,
                     slab_ref):
    tm = x_ref.shape[0]
    heads = N_KV_HEADS
    half = ROT_DIMS // 2
    kvw = heads * HEAD_DIM
    x = x_ref[...].astype(BF16)
    xb_ref[...] = x

    ang = pos_ref[...].astype(F32) * invf_ref[...]
    lane = lax.broadcasted_iota(jnp.int32, (tm, V7X_LANES), 1)
    per_row = V7X_LANES // half

    def spread(packed, fill):
        rows = jnp.repeat(packed, per_row, axis=0)
        lo = pltpu.roll(rows, 0, 1, stride=half, stride_axis=0)
        return jnp.where(lane < half, lo, jnp.where(lane < ROT_DIMS, pltpu.roll(lo, half, 1), fill))

    cos = spread(jnp.cos(ang), 1.0)
    sin = spread(jnp.sin(ang), 0.0)
    sin_lo = jnp.where(lane < half, -sin, 0.0)
    sin_hi = jnp.where(lane >= half, sin, 0.0)

    def rotate(t):
        return t * cos + pltpu.roll(t, HEAD_DIM - half, 1) * sin_lo + pltpu.roll(t, half, 1) * sin_hi

    def write_units(t, h, slab, o1_ref, o4_ref, o16_ref):
        if o1_ref is not None:
            for jb in range(tm // ATTN_SPAN):
                o1_ref[jb, h] = t[jb * ATTN_SPAN:(jb + 1) * ATTN_SPAN].astype(BF16)
        if o4_ref is None and o16_ref is None:
            return
        d4 = DILATIONS[1]
        slab_ref[0, slab] = t
        for r4 in range(d4):
            rows4 = slab_ref[0, slab, pl.ds(r4, tm // d4, stride=d4), :]
            if o4_ref is not None:
                o4_ref[r4, h] = rows4.astype(BF16)
            if o16_ref is not None:
                slab_ref[1, slab, r4 * (tm // d4):(r4 + 1) * (tm // d4), :] = rows4
        if o16_ref is not None:
            d16 = DILATIONS[2]
            for r16 in range(d16):
                r4, c = r16 % d4, r16 // d4
                start = r4 * (tm // d4) + c
                o16_ref[r16, h] = slab_ref[1, slab, pl.ds(start, tm // d16, stride=d16 // d4), :].astype(BF16)

    groups = (
        (0, True, (q1_ref, None, None)), (1, True, (None, q4_ref, None)), (2, True, (None, None, q16_ref)),
        (3, True, (k1_ref, k4_ref, k16_ref)), (4, False, (v1_ref, v4_ref, v16_ref)),
    )
    for step, (j, rot, outs) in enumerate(groups):
        acc = jnp.dot(x, w_ref[:, j * kvw:(j + 1) * kvw], preferred_element_type=F32)
        for h in range(heads):
            t = acc[:, h * HEAD_DIM:(h + 1) * HEAD_DIM]
            write_units(rotate(t) if rot else t, h, (step * heads + h) % slab_ref.shape[1], *outs)


def _qkv_proj(x, pos, invf, w, *, n):
    m, d = x.shape
    tm = QKV_ROWS
    heads = N_KV_HEADS
    per_tile = ATTN_TILE // tm
    assert m % ATTN_TILE == 0 and n == 5 * heads * HEAD_DIM and w.shape[1] >= n
    units = lambda rows: jax.ShapeDtypeStruct((rows // ATTN_SPAN, heads, ATTN_SPAN, HEAD_DIM), BF16)
    flat_spec = pl.BlockSpec((tm // ATTN_SPAN, heads, ATTN_SPAN, HEAD_DIM), lambda i: (i, 0, 0, 0))
    wide_spec = pl.BlockSpec((DILATIONS[2], heads, tm // DILATIONS[2], HEAD_DIM),
                             lambda i: (i // per_tile, 0, i % per_tile, 0))
    out_specs = [flat_spec, flat_spec, wide_spec] * 3 + [pl.BlockSpec((tm, d), lambda i: (i, 0))]
    n_slabs = 4
    vmem = (2 * tm * d * 4 + tm * d * 2 + 2 * d * n * 2 + 2 * 9 * tm * heads * HEAD_DIM * 2 + 2 * tm * d * 2
            + 2 * n_slabs * tm * HEAD_DIM * 4 + 8 * tm * heads * HEAD_DIM * 4 + (6 << 20))
    return pl.pallas_call(
        _qkv_proj_kernel,
        grid=(m // tm,),
        in_specs=[
            pl.BlockSpec((tm * (ROT_DIMS // 2) // V7X_LANES, V7X_LANES), lambda i: (i, 0)),
            pl.BlockSpec((1, V7X_LANES), lambda i: (0, 0)),
            pl.BlockSpec((tm, d), lambda i: (i, 0)),
            pl.BlockSpec((d, n), lambda i: (0, 0)),
        ],
        out_specs=out_specs,
        out_shape=[units(m)] * 9 + [jax.ShapeDtypeStruct((m, d), BF16)],
        scratch_shapes=[pltpu.VMEM((2, n_slabs, tm, HEAD_DIM), F32)],
        compiler_params=_compiler_params(("parallel",), vmem),
        name="qkv_proj",
    )(pos, invf, x, w)


def _attn_kernel(q1_ref, q4_ref, q16_ref, k1c_ref, k4c_ref, k16c_ref, k1p_ref, k4p_ref, k16p_ref,
                 v1c_ref, v4c_ref, v16c_ref, v1p_ref, v4p_ref, v16p_ref, o_ref, acc_ref, lse_ref, stage_ref):
    first_tile = pl.program_id(2) == 0
    blk = ATTN_SPAN
    nu = ATTN_UNITS
    scale = HEAD_DIM ** -0.5
    qi = lax.broadcasted_iota(jnp.int32, (1, blk, blk), 1)
    ki = lax.broadcasted_iota(jnp.int32, (1, blk, blk), 2)
    unit = lax.broadcasted_iota(jnp.int32, (nu, 1, 1), 0)
    neg_inf = -jnp.inf
    qk = lambda a, b: jnp.einsum("uqd,ukd->uqk", a, b, preferred_element_type=F32)
    pv = lambda a, b: jnp.einsum("uqk,ukd->uqd", a, b, preferred_element_type=F32)

    def with_prev(fn, lhs, prev_ref, cur_ref, d, hh, a, b):
        parts = []
        if a < d:
            parts.append(fn(lhs[:min(b, d) - a], prev_ref[a:min(b, d), hh]))
        if b > d:
            lo = max(a, d)
            parts.append(fn(lhs[lo - a:], cur_ref[lo - d:b - d, hh]))
        return parts[0] if len(parts) == 1 else jnp.concatenate(parts, axis=0)

    refs = ((q1_ref, k1c_ref, k1p_ref, v1c_ref, v1p_ref), (q4_ref, k4c_ref, k4p_ref, v4c_ref, v4p_ref),
            (q16_ref, k16c_ref, k16p_ref, v16c_ref, v16p_ref))
    halves = ((0, nu // 2), (nu // 2, nu))
    for hh in range(q1_ref.shape[1]):
        for p, (d, (q_ref, kc_ref, kp_ref, vc_ref, vp_ref)) in enumerate(zip(DILATIONS, refs)):
          for a, b in halves:
            q = q_ref[a:b, hh]
            sp = with_prev(qk, q, kp_ref, kc_ref, d, hh, a, b) * scale
            sc = qk(q, kc_ref[a:b, hh]) * scale
            sp = jnp.where(ki >= qi, sp, neg_inf)
            sp = jnp.where(jnp.logical_and(first_tile, unit[a:b] < d), neg_inf, sp)
            sc = jnp.where(ki <= qi, sc, neg_inf)
            mx = jnp.max(jnp.maximum(sp, sc), axis=-1, keepdims=True)
            ep = jnp.exp(sp - mx)
            ec = jnp.exp(sc - mx)
            den = jnp.sum(ep + ec, axis=-1, keepdims=True)
            out = (with_prev(pv, ep.astype(BF16), vp_ref, vc_ref, d, hh, a, b)
                   + pv(ec.astype(BF16), vc_ref[a:b, hh])) / den
            lse = jnp.broadcast_to(mx + jnp.log(den), out.shape)
            d4 = DILATIONS[1]
            if d == 1:
                acc_ref[hh, p, a * blk:b * blk] = out.reshape((b - a) * blk, HEAD_DIM)
                lse_ref[hh, p, a * blk:b * blk] = lse.reshape((b - a) * blk, HEAD_DIM)
            elif d == d4:
                for u in range(a, b):
                    jb, r = divmod(u, d)
                    rows = pl.ds(d * blk * jb + r, blk, stride=d)
                    acc_ref[hh, p, rows, :] = out[u - a]
                    lse_ref[hh, p, rows, :] = lse[u - a]
            else:
                per = ATTN_TILE // d4
                for u in range(a, b):
                    rows = pl.ds(u // d4, blk, stride=d4)
                    stage_ref[0, u % d4, rows, :] = out[u - a]
                    stage_ref[1, u % d4, rows, :] = lse[u - a]
                if b == nu:
                    for r4 in range(d4):
                        acc_ref[hh, p, pl.ds(r4, per, stride=d4), :] = stage_ref[0, r4]
                        lse_ref[hh, p, pl.ds(r4, per, stride=d4), :] = stage_ref[1, r4]

        lse = lse_ref[hh]
        w = jnp.exp(lse - jnp.max(lse, axis=0, keepdims=True))
        merged = jnp.sum(w * acc_ref[hh], axis=0) / jnp.sum(w, axis=0)
        o_ref[:, hh * HEAD_DIM:(hh + 1) * HEAD_DIM] = merged.astype(o_ref.dtype)


def _attention(qkv, *, batch, seq, heads_per_step=2):
    nt = seq // ATTN_TILE
    heads = N_KV_HEADS
    nu = ATTN_UNITS
    hps = heads_per_step
    assert heads % hps == 0
    q1, q4, q16, k1, k4, k16, v1, v4, v16 = qkv

    def cur():
        return pl.BlockSpec((nu, hps, ATTN_SPAN, HEAD_DIM), lambda b, h, n: (b * nt + n, h, 0, 0))

    def prev(d):
        per = nu // d
        return pl.BlockSpec((d, hps, ATTN_SPAN, HEAD_DIM),
                            lambda b, h, n: (jnp.maximum((b * nt + n) * per - 1, 0), h, 0, 0))

    unit_bytes = ATTN_SPAN * HEAD_DIM * 2
    tile_f32 = ATTN_TILE * HEAD_DIM * 4
    vmem = (hps * (2 * (9 * nu + 2 * sum(DILATIONS)) * unit_bytes + 2 * tile_f32 // 2 + 6 * tile_f32)
            + 24 * tile_f32 + (4 << 20))
    return pl.pallas_call(
        _attn_kernel,
        grid=(batch, heads // hps, nt),
        in_specs=[cur(), cur(), cur(), cur(), cur(), cur(), prev(1), prev(4), prev(16),
                  cur(), cur(), cur(), prev(1), prev(4), prev(16)],
        out_specs=pl.BlockSpec((ATTN_TILE, hps * HEAD_DIM), lambda b, h, n: (b * nt + n, h)),
        out_shape=jax.ShapeDtypeStruct((batch * seq, heads * HEAD_DIM), BF16),
        scratch_shapes=[pltpu.VMEM((hps, len(DILATIONS), ATTN_TILE, HEAD_DIM), F32)] * 2
        + [pltpu.VMEM((2, DILATIONS[1], ATTN_TILE // DILATIONS[1], HEAD_DIM), F32)],
        compiler_params=_compiler_params(("parallel", "parallel", "arbitrary"), vmem),
        name="dilated_attention",
    )(q1, q4, q16, k1, k4, k16, k1, k4, k16, v1, v4, v16, v1, v4, v16)


SCAN_PAD_ROWS = V7X_SUBLANES


def _linear_scan_rows(a, u, h0, sa_ref, su_ref):
    rows, width = a.shape
    sub = V7X_SUBLANES
    run = rows // sub
    pitch = run + SCAN_PAD_ROWS
    lanes = V7X_LANES
    out = []
    carry = []
    for c in range(width // lanes):
        cols = slice(c * lanes, (c + 1) * lanes)
        for s in range(sub):
            sa_ref[c, s * pitch:s * pitch + run, :] = a[s * run:(s + 1) * run, cols]
            su_ref[c, s * pitch:s * pitch + run, :] = u[s * run:(s + 1) * run, cols]
        h = jnp.zeros((sub, lanes), F32)
        prod = jnp.ones((sub, lanes), F32)
        hs, prods = [], []
        for k in range(run):
            ak = sa_ref[c, pl.ds(k, sub, stride=pitch), :]
            uk = su_ref[c, pl.ds(k, sub, stride=pitch), :]
            h = ak * h + uk
            prod = ak * prod
            hs.append(h)
            prods.append(prod)
        start = h0[:, cols]
        enter = [start]
        for s in range(sub - 1):
            start = h[s:s + 1, :] + prod[s:s + 1, :] * start
            enter.append(start)
        carry.append(h[sub - 1:sub, :] + prod[sub - 1:sub, :] * start)
        enter = jnp.concatenate(enter, axis=0)
        for k in range(run):
            su_ref[c, pl.ds(k, sub, stride=pitch), :] = hs[k] + prods[k] * enter
        out.append(jnp.concatenate([su_ref[c, s * pitch:s * pitch + run, :] for s in range(sub)], axis=0))
    return jnp.concatenate(out, axis=1), jnp.concatenate(carry, axis=1)


def _gelu_tanh(y):
    z = 0.7978845608028654 * (y + 0.044715 * (y * y * y))
    return y * jax.nn.sigmoid(2.0 * z)


def _lru_kernel(hin_ref, *refs, nc):
    wx_refs, wy_refs = refs[:nc], refs[nc:2 * nc]
    cw_ref, cb_ref, wri_ref, br_ref, bi_ref, lam_ref, o_ref, xs_ref, ys_ref, h_ref, sa_ref, su_ref = refs[2 * nc:]
    tl = hin_ref.shape[0]
    tc = wx_refs[0].shape[1]
    pad = V7X_SUBLANES
    s = pl.program_id(1)

    @pl.when(s == 0)
    def _():
        xs_ref[...] = jnp.zeros_like(xs_ref)
        ys_ref[...] = jnp.zeros_like(ys_ref)

    @pl.when(s <= 1)
    def _():
        xs_ref[:, 0:pad, :] = jnp.zeros((nc, pad, tc), F32)
        h_ref[...] = jnp.zeros_like(h_ref)

    half = tc // 2
    bw = LRU_BLOCK_WIDTH
    gpc = tc // bw
    for c in range(nc):
        ch = slice(c * tc, (c + 1) * tc)

        def project(dst_ref, rows, w_ref, part):
            cols = slice(part * half, (part + 1) * half)
            dst_ref[c, rows, cols] = jnp.dot(hin_ref[...], w_ref[:, cols], preferred_element_type=F32)

        x = xs_ref[c, pad:pad + tl, :]
        y = ys_ref[c]
        project(ys_ref, slice(0, tl), wy_refs[c], 0)
        xc = cw_ref[CONV_WIDTH - 1:CONV_WIDTH, ch] * x
        for j in range(CONV_WIDTH - 1):
            shift = CONV_WIDTH - 1 - j
            xc = xc + cw_ref[j:j + 1, ch] * xs_ref[c, pad - shift:pad - shift + tl, :]
        xc = xc + cb_ref[:, ch]
        xs_ref[c, 0:pad, :] = x[tl - pad:tl]

        pre = [jnp.dot(xc[:, g * bw:(g + 1) * bw].astype(BF16), wri_ref[c * gpc + g], preferred_element_type=F32)
               for g in range(gpc)]
        project(ys_ref, slice(0, tl), wy_refs[c], 1)
        project(xs_ref, slice(pad, pad + tl), wx_refs[c], 0)

        r = jax.nn.sigmoid(jnp.concatenate([t[:, :bw] for t in pre], axis=1) + br_ref[:, ch])
        i = jax.nn.sigmoid(jnp.concatenate([t[:, bw:] for t in pre], axis=1) + bi_ref[:, ch])
        neg_lam = -lam_ref[:, ch]
        softplus = jnp.maximum(neg_lam, 0.0) + jnp.log1p(jnp.exp(-jnp.abs(neg_lam)))
        log_a = (-LRU_C * softplus) * r
        a = jnp.exp(log_a)
        one_m_a2 = 1.0 - a * a
        root = jnp.where(one_m_a2 > 0.0, one_m_a2 * lax.rsqrt(one_m_a2), 0.0)
        u = root * (i * xc)
        project(xs_ref, slice(pad, pad + tl), wx_refs[c], 1)
        h, h_ref[c] = _linear_scan_rows(a, u, h_ref[c], sa_ref.at[c], su_ref.at[c])
        o_ref[:, ch] = (h * _gelu_tanh(y)).astype(o_ref.dtype)


def _rg_lru(hin, w_in, conv_w, conv_b, w_ri, b_r, b_i, lam, *, batch, seq, x_col, width, tl=256, tc=512):
    m, d = hin.shape
    nt = seq // tl
    nc = width // tc
    assert seq % tl == 0 and width % tc == 0 and x_col % tc == 0
    whole = lambda shape: pl.BlockSpec(shape, lambda b, n: (0,) * len(shape), pipeline_mode=pl.Buffered(1))
    w_block = lambda col: pl.BlockSpec((d, tc), lambda b, n: (0, col), pipeline_mode=pl.Buffered(1))
    vmem = (2 * tl * d * 2 + 2 * nc * d * tc * 2 + 2 * tl * width * 2 + width * 256 * 2
            + (4 * tl + 6 * V7X_SUBLANES * SCAN_PAD_ROWS) * width * 4 + 16 * tl * tc * 4 + (4 << 20))
    return pl.pallas_call(
        functools.partial(_lru_kernel, nc=nc),
        grid=(batch, nt + 1),
        in_specs=[pl.BlockSpec((tl, d), lambda b, n: (b * nt + jnp.minimum(n, nt - 1), 0))]
        + [w_block(x_col // tc + c) for c in range(2 * nc)]
        + [whole((CONV_WIDTH, width)), whole((1, width)),
           whole((width // LRU_BLOCK_WIDTH, LRU_BLOCK_WIDTH, 2 * LRU_BLOCK_WIDTH)),
           whole((1, width)), whole((1, width)), whole((1, width))],
        out_specs=pl.BlockSpec((tl, width), lambda b, n: (b * nt + jnp.maximum(n - 1, 0), 0)),
        out_shape=jax.ShapeDtypeStruct((m, width), BF16),
        scratch_shapes=[pltpu.VMEM((nc, tl + V7X_SUBLANES, tc), F32), pltpu.VMEM((nc, tl, tc), F32),
                        pltpu.VMEM((nc, 1, tc), F32)]
        + [pltpu.VMEM((nc, tc // V7X_LANES, tl + V7X_SUBLANES * SCAN_PAD_ROWS, V7X_LANES), F32)] * 2,
        compiler_params=_compiler_params(("parallel", "arbitrary"), vmem),
        name="rg_lru",
    )(hin, *([w_in] * (2 * nc)), conv_w, conv_b, w_ri, b_r, b_i, lam)


def _resident(shape):
    return pl.BlockSpec(shape, lambda i: (0,) * len(shape), pipeline_mode=pl.Buffered(1))


def _out_ln_kernel(x_ref, a_ref, r_ref, w_ref, g_ref, b_ref, o_ref, wb_ref, *, alpha, row_chunks):
    @pl.when(pl.program_id(0) == 0)
    def _():
        wb_ref[...] = w_ref[...].astype(BF16)

    rows = x_ref.shape[0] // row_chunks
    ka = a_ref.shape[1]
    for c in range(row_chunks):
        sl = slice(c * rows, (c + 1) * rows)
        mix = (jnp.dot(a_ref[sl, :], wb_ref[:ka, :], preferred_element_type=F32)
               + jnp.dot(r_ref[sl, :], wb_ref[ka:, :], preferred_element_type=F32))
        o_ref[sl, :] = _layer_norm_rows(alpha * x_ref[sl, :] + mix, g_ref[...], b_ref[...])


def _out_ln(x, attn, rec, w, g, b, *, alpha, tm=512, row_chunks=2):
    m, d = x.shape
    ka, kr = attn.shape[1], rec.shape[1]
    assert m % tm == 0 and tm % row_chunks == 0 and w.shape == (ka + kr, d)
    vmem = (2 * 2 * tm * d * 4 + 2 * tm * (ka + kr) * 2 + (ka + kr) * d * (4 + 2) + 3 * tm * d * 4 + (4 << 20))
    return pl.pallas_call(
        functools.partial(_out_ln_kernel, alpha=alpha, row_chunks=row_chunks),
        grid=(m // tm,),
        in_specs=[
            pl.BlockSpec((tm, d), lambda i: (i, 0)),
            pl.BlockSpec((tm, ka), lambda i: (i, 0)),
            pl.BlockSpec((tm, kr), lambda i: (i, 0)),
            _resident((ka + kr, d)), _resident((1, d)), _resident((1, d)),
        ],
        out_specs=pl.BlockSpec((tm, d), lambda i: (i, 0)),
        out_shape=jax.ShapeDtypeStruct((m, d), F32),
        scratch_shapes=[pltpu.VMEM((ka + kr, d), BF16)],
        compiler_params=_compiler_params(("arbitrary",), vmem),
        name="out_ln",
    )(x, attn, rec, w, g, b)


def _ple_kernel(x_ref, p_ref, wg_ref, wp_ref, o_ref, wgb_ref, wpb_ref):
    @pl.when(pl.program_id(0) == 0)
    def _():
        wgb_ref[...] = wg_ref[...].astype(BF16)
        wpb_ref[...] = wp_ref[...].astype(BF16)

    x = x_ref[...]
    gate = jax.nn.sigmoid(jnp.dot(x.astype(BF16), wgb_ref[...], preferred_element_type=F32))
    emb = jnp.dot(p_ref[...].astype(BF16), wpb_ref[...], preferred_element_type=F32)
    o_ref[...] = x + gate * emb


def _ple(x, p, wg, wp, *, tm=512):
    m, d = x.shape
    dp = p.shape[1]
    assert m % tm == 0
    vmem = 2 * 2 * tm * d * 4 + 2 * tm * dp * 4 + (d + dp) * d * (4 + 2) + 3 * tm * d * 4 + (4 << 20)
    return pl.pallas_call(
        _ple_kernel,
        grid=(m // tm,),
        in_specs=[
            pl.BlockSpec((tm, d), lambda i: (i, 0)),
            pl.BlockSpec((tm, dp), lambda i: (i, 0)),
            _resident((d, d)), _resident((dp, d)),
        ],
        out_specs=pl.BlockSpec((tm, d), lambda i: (i, 0)),
        out_shape=jax.ShapeDtypeStruct((m, d), F32),
        scratch_shapes=[pltpu.VMEM((d, d), BF16), pltpu.VMEM((dp, d), BF16)],
        compiler_params=_compiler_params(("arbitrary",), vmem),
        name="ple",
    )(x, p, wg, wp)


def kernel(x, p, positions, ffn1_w_gate, ffn1_w_up, ffn1_w_down, ln1_g, ln1_b, w_in, conv_w, conv_b, w_rgate, b_rgate, w_igate, b_igate, lru_lambda, w_out, ln2_g, ln2_b, ffn2_w_gate, ffn2_w_up, ffn2_w_down, ln3_g, ln3_b, w_ple_proj, w_ple_gate):
    batch, seq, d_model = x.shape
    depth = ffn1_w_gate.shape[0]
    m = batch * seq
    kv_width = N_KV_HEADS * HEAD_DIM
    qkv_width = (len(DILATIONS) + 2) * kv_width
    lru_width = d_model - kv_width
    alpha = (2.0 * depth) ** 0.25
    assert w_in.shape[2] == qkv_width + 2 * lru_width

    half = ROT_DIMS // 2
    inv_freq = jnp.power(jnp.float32(ROPE_THETA), -jnp.arange(half, dtype=F32) * (2.0 / ROT_DIMS))
    per_row = V7X_LANES // half
    invf = jnp.tile(inv_freq, per_row).reshape(1, V7X_LANES)
    order = jnp.array([(per_row - b) % per_row for b in range(per_row)])
    pos = jnp.repeat(positions.reshape(m // per_row, per_row)[:, order], half, axis=1)
    row = lambda v: v.reshape(1, -1)

    h = x.reshape(m, d_model)
    for i in range(depth):
        h, (*ffn2_w, w_in_bf16) = _ffn_ln(
            h, ffn1_w_gate[i].astype(BF16), ffn1_w_up[i].astype(BF16), ffn1_w_down[i].astype(BF16),
            row(ln1_g[i]), row(ln1_b[i]), alpha=alpha, layer=i,
            cast_payload=(ffn2_w_gate, ffn2_w_up, ffn2_w_down, w_in))
        *qkv, hb = _qkv_proj(h, pos, invf, w_in_bf16, n=qkv_width)
        attn = _attention(qkv, batch=batch, seq=seq)
        w_ri = jnp.concatenate([w_rgate[i], w_igate[i]], axis=-1).astype(BF16)
        rec = _rg_lru(hb, w_in_bf16, conv_w[i], row(conv_b[i]), w_ri, row(b_rgate[i]), row(b_igate[i]),
                      row(lru_lambda[i]), batch=batch, seq=seq, x_col=qkv_width, width=lru_width)
        h = _out_ln(h, attn, rec, w_out[i], row(ln2_g[i]), row(ln2_b[i]), alpha=alpha)
        h, _ = _ffn_ln(h, *ffn2_w, row(ln3_g[i]), row(ln3_b[i]), alpha=alpha)
        h = _ple(h, p[i].reshape(m, -1), w_ple_gate[i], w_ple_proj[i])
    return h.reshape(batch, seq, d_model)
```
